```python
import jax, jax.numpy as jnp
from jax import lax
import numpy as np

D_MODEL = 2048
BATCH = 16
SEQ = 2048
DEPTH = 1
DEC_BATCH = 2
DEC_SEQ = 8192
PAST_LEN = 128

PLE_DIM = 256
MIX_WIDTH = D_MODEL
POOL_WIDTH = MIX_WIDTH // 2
POOL_WINDOWS = (2, 4, 8, 16)
N_POOL_GROUPS = 4
POOL_GROUP = POOL_WIDTH // N_POOL_GROUPS
RWKV_WIDTH = MIX_WIDTH - POOL_WIDTH
HEAD_SIZE = 64
N_RWKV_HEADS = RWKV_WIDTH // HEAD_SIZE
DECAY_LORA = 64
ICL_LORA = 64
GATE_LORA = 160
FFN_DIM = 5632
SHIFT_COLS = 3 * RWKV_WIDTH + DECAY_LORA + ICL_LORA + GATE_LORA
IN_COLS = POOL_WIDTH + SHIFT_COLS
RWKV_SPLITS = (RWKV_WIDTH, 2 * RWKV_WIDTH, 3 * RWKV_WIDTH,
               3 * RWKV_WIDTH + DECAY_LORA, 3 * RWKV_WIDTH + DECAY_LORA + ICL_LORA)
RMS_EPS = 1e-6
LNX_EPS = 64e-5

kernel_name = "pool_rwkv7_hybrid_encoder"


def _rmsnorm(x, g):
    xf = x.astype(jnp.float32)
    y = xf * lax.rsqrt(jnp.mean(xf * xf, axis=-1, keepdims=True) + RMS_EPS)
    return (y * g.astype(jnp.float32)).astype(x.dtype)


def _swiglu(u, w_gate, w_up, w_down):
    return (jax.nn.silu(u @ w_gate) * (u @ w_up)) @ w_down


def _centred_mean_minus_self(xg, window):
    L = xg.shape[1]
    xf = xg.astype(jnp.float32)
    cs = jnp.pad(jnp.cumsum(xf, axis=1), ((0, 0), (1, 0), (0, 0)))
    t = jnp.arange(L)
    lo = jnp.clip(t - window // 2, 0, L)
    hi = jnp.clip(t + (window - window // 2), 0, L)
    total = jnp.take(cs, hi, axis=1) - jnp.take(cs, lo, axis=1)
    count = (hi - lo).astype(jnp.float32)[None, :, None]
    return (total / count - xf).astype(xg.dtype)


def _pool_mixer(z, pool_w, pool_scale):
    B, L, _ = z.shape
    zg = z.reshape(B, L, N_POOL_GROUPS, POOL_GROUP)
    pooled = jnp.stack([_centred_mean_minus_self(zg[:, :, g], POOL_WINDOWS[g])
                        for g in range(N_POOL_GROUPS)], axis=2)
    mixed = jnp.einsum('blgc,gcd->blgd', pooled, pool_w)
    return mixed.reshape(B, L, POOL_WIDTH) * pool_scale


def _token_shift(z, mu):
    zp = jnp.pad(z, ((0, 0), (1, 1), (0, 0)))
    nb = 0.5 * (zp[:, :-2] + zp[:, 2:])
    return (z + mu * (nb - z)).astype(z.dtype)


def _heads(t):
    return t.reshape(t.shape[0], t.shape[1], N_RWKV_HEADS, HEAD_SIZE)


def _wkv_scan(r, w, k, v, a, b, reverse):
    def step(S, inp):
        r_t, w_t, k_t, v_t, a_t, b_t = inp
        sa = jnp.einsum('bhij,bhj->bhi', S, a_t)
        S = (S * w_t[:, :, None, :] + sa[..., :, None] * b_t[..., None, :]
             + v_t[..., :, None] * k_t[..., None, :])
        y = jnp.einsum('bhij,bhj->bhi', S, r_t)
        return S, y
    B = r.shape[0]
    xs = tuple(jnp.moveaxis(t, 1, 0) for t in (r, w, k, v, a, b))
    S0 = jnp.zeros((B, N_RWKV_HEADS, HEAD_SIZE, HEAD_SIZE), jnp.float32)
    _, ys = lax.scan(step, S0, xs, reverse=reverse)
    return jnp.moveaxis(ys, 0, 1)


def _rwkv7_mixer(feats, w0_f, w_up_f, a0_f, a_up_f, w0_b, w_up_b, a0_b, a_up_b,
                 g_up, k_k, k_a, r_k, lnx_w, lnx_b):
    f32 = jnp.float32
    B, L, _ = feats.shape
    r, k, v, dw, da, dg = jnp.split(feats, RWKV_SPLITS, axis=-1)
    g = (jax.nn.sigmoid(dg) @ g_up).astype(f32)
    tw = jnp.tanh(dw)
    rh = _heads(r.astype(f32))
    vh = _heads(v.astype(f32))
    kf = k.astype(f32)
    kk = _heads(kf * k_k.astype(f32))
    kk = kk / jnp.maximum(jnp.sqrt(jnp.sum(kk * kk, axis=-1, keepdims=True)), 1e-12)

    def direction(w0, w_up, a0, a_up, reverse):
        wlog = -jax.nn.softplus(-(w0.astype(f32) + (tw @ w_up).astype(f32))) - 0.5
        decay = jnp.exp(-jnp.exp(wlog))
        a = jax.nn.sigmoid(a0.astype(f32) + (da @ a_up).astype(f32))
        kd = _heads(kf * (1.0 + (a - 1.0) * k_a.astype(f32)))
        y = _wkv_scan(rh, _heads(decay), kd, vh, -kk, kk * _heads(a), reverse)
        return y, kd

    y_f, k_f = direction(w0_f, w_up_f, a0_f, a_up_f, False)
    y_b, k_b = direction(w0_b, w_up_b, a0_b, a_up_b, True)
    y = y_f + y_b
    mean = jnp.mean(y, axis=-1, keepdims=True)
    var = jnp.mean(jnp.square(y - mean), axis=-1, keepdims=True)
    y = ((y - mean) * lax.rsqrt(var + LNX_EPS)).reshape(B, L, RWKV_WIDTH)
    y = y * lnx_w.astype(f32) + lnx_b.astype(f32)
    bonus = jnp.sum(rh * (k_f + k_b) * r_k.astype(f32), axis=-1, keepdims=True) * vh
    out = (y + bonus.reshape(B, L, RWKV_WIDTH)) * g
    return out.astype(feats.dtype)


def _layer(x, p, ffn1_pre_g, ffn1_post_g, ffn1_w_gate, ffn1_w_up, ffn1_w_down,
           mix_pre_g, mix_post_g, w_in, mu_shift, pool_w, pool_scale,
           w0_f, w_up_f, a0_f, a_up_f, w0_b, w_up_b, a0_b, a_up_b,
           g_up, k_k, k_a, r_k, lnx_w, lnx_b, w_out,
           ffn2_pre_g, ffn2_post_g, ffn2_w_gate, ffn2_w_up, ffn2_w_down,
           ple_pre_g, ple_post_g, ple_gate_w, ple_proj_w):
    h = _swiglu(_rmsnorm(x, ffn1_pre_g), ffn1_w_gate, ffn1_w_up, ffn1_w_down)
    x = x + 0.5 * _rmsnorm(h, ffn1_post_g)
    u = _rmsnorm(x, mix_pre_g)
    z = u @ w_in
    pool_out = _pool_mixer(z[..., :POOL_WIDTH], pool_w, pool_scale)
    feats = _token_shift(z[..., POOL_WIDTH:], mu_shift)
    rwkv_out = _rwkv7_mixer(feats, w0_f, w_up_f, a0_f, a_up_f, w0_b, w_up_b, a0_b, a_up_b,
                            g_up, k_k, k_a, r_k, lnx_w, lnx_b)
    mixed = jnp.concatenate([pool_out, rwkv_out], axis=-1) @ w_out
    x = x + _rmsnorm(mixed, mix_post_g)
    h = _swiglu(_rmsnorm(x, ffn2_pre_g), ffn2_w_gate, ffn2_w_up, ffn2_w_down)
    x = x + 0.5 * _rmsnorm(h, ffn2_post_g)
    gate = jax.nn.sigmoid(_rmsnorm(x, ple_pre_g) @ ple_gate_w)
    x = x + _rmsnorm(gate * (p @ ple_proj_w), ple_post_g)
    return x


def setup_inputs(seed: int = 0) -> dict:
    key = jax.random.key(seed)
    ks = jax.random.split(key, 39)
    f32 = jnp.float32

    def nrm(k, shape, scale):
        return scale * jax.random.normal(k, shape, f32)

    def gain(k, shape):
        return 1.0 + 0.05 * jax.random.normal(k, shape, f32)

    C = RWKV_WIDTH
    ramp = -5.0 + 4.0 * jnp.arange(C, dtype=f32) / (C - 1)
    return {
        "x_prompt": nrm(ks[0], (BATCH, SEQ, D_MODEL), 1.0),
        "x_sample": nrm(ks[1], (DEC_BATCH, DEC_SEQ, D_MODEL), 1.0),
        "p_prompt": nrm(ks[2], (DEPTH, BATCH, SEQ, PLE_DIM), 1.0),
        "p_sample": nrm(ks[3], (DEPTH, DEC_BATCH, DEC_SEQ, PLE_DIM), 1.0),
        "ffn1_pre_g": gain(ks[4], (DEPTH, D_MODEL)),
        "ffn1_post_g": gain(ks[5], (DEPTH, D_MODEL)),
        "ffn1_w_gate": nrm(ks[6], (DEPTH, D_MODEL, FFN_DIM), D_MODEL ** -0.5),
        "ffn1_w_up": nrm(ks[7], (DEPTH, D_MODEL, FFN_DIM), D_MODEL ** -0.5),
        "ffn1_w_down": nrm(ks[8], (DEPTH, FFN_DIM, D_MODEL), FFN_DIM ** -0.5),
        "mix_pre_g": gain(ks[9], (DEPTH, D_MODEL)),
        "mix_post_g": gain(ks[10], (DEPTH, D_MODEL)),
        "w_in": nrm(ks[11], (DEPTH, D_MODEL, IN_COLS), D_MODEL ** -0.5),
        "mu_shift": jax.random.uniform(ks[12], (DEPTH, SHIFT_COLS), f32),
        "pool_w": nrm(ks[13], (DEPTH, N_POOL_GROUPS, POOL_GROUP, POOL_GROUP), POOL_GROUP ** -0.5),
        "pool_scale": 1.0 + 0.1 * jax.random.normal(ks[14], (DEPTH, POOL_WIDTH), f32),
        "w0_f": ramp + nrm(ks[15], (DEPTH, C), 0.1),
        "w_up_f": nrm(ks[16], (DEPTH, DECAY_LORA, C), 0.1 * DECAY_LORA ** -0.5),
        "a0_f": nrm(ks[17], (DEPTH, C), 0.1),
        "a_up_f": nrm(ks[18], (DEPTH, ICL_LORA, C), 0.1 * ICL_LORA ** -0.5),
        "w0_b": ramp + nrm(ks[19], (DEPTH, C), 0.1),
        "w_up_b": nrm(ks[20], (DEPTH, DECAY_LORA, C), 0.1 * DECAY_LORA ** -0.5),
        "a0_b": nrm(ks[21], (DEPTH, C), 0.1),
        "a_up_b": nrm(ks[22], (DEPTH, ICL_LORA, C), 0.1 * ICL_LORA ** -0.5),
        "g_up": nrm(ks[23], (DEPTH, GATE_LORA, C), GATE_LORA ** -0.5),
        "k_k": 0.85 + 0.05 * jax.random.normal(ks[24], (DEPTH, C), f32),
        "k_a": gain(ks[25], (DEPTH, C)),
        "r_k": nrm(ks[26], (DEPTH, N_RWKV_HEADS, HEAD_SIZE), 0.1),
        "lnx_w": gain(ks[27], (DEPTH, C)),
        "lnx_b": nrm(ks[28], (DEPTH, C), 0.02),
        "w_out": nrm(ks[29], (DEPTH, MIX_WIDTH, D_MODEL), MIX_WIDTH ** -0.5),
        "ffn2_pre_g": gain(ks[30], (DEPTH, D_MODEL)),
        "ffn2_post_g": gain(ks[31], (DEPTH, D_MODEL)),
        "ffn2_w_gate": nrm(ks[32], (DEPTH, D_MODEL, FFN_DIM), D_MODEL ** -0.5),
        "ffn2_w_up": nrm(ks[33], (DEPTH, D_MODEL, FFN_DIM), D_MODEL ** -0.5),
        "ffn2_w_down": nrm(ks[34], (DEPTH, FFN_DIM, D_MODEL), FFN_DIM ** -0.5),
        "ple_pre_g": gain(ks[35], (DEPTH, D_MODEL)),
        "ple_post_g": gain(ks[36], (DEPTH, D_MODEL)),
        "ple_gate_w": nrm(ks[37], (DEPTH, D_MODEL, D_MODEL), D_MODEL ** -0.5),
        "ple_proj_w": nrm(ks[38], (DEPTH, PLE_DIM, D_MODEL), PLE_DIM ** -0.5),
    }


def reference(x_prompt, x_sample, p_prompt, p_sample,
              ffn1_pre_g, ffn1_post_g, ffn1_w_gate, ffn1_w_up, ffn1_w_down,
              mix_pre_g, mix_post_g, w_in, mu_shift, pool_w, pool_scale,
              w0_f, w_up_f, a0_f, a_up_f, w0_b, w_up_b, a0_b, a_up_b,
              g_up, k_k, k_a, r_k, lnx_w, lnx_b, w_out,
              ffn2_pre_g, ffn2_post_g, ffn2_w_gate, ffn2_w_up, ffn2_w_down,
              ple_pre_g, ple_post_g, ple_gate_w, ple_proj_w):
    weights = (ffn1_pre_g, ffn1_post_g, ffn1_w_gate, ffn1_w_up, ffn1_w_down,
               mix_pre_g, mix_post_g, w_in, mu_shift, pool_w, pool_scale,
               w0_f, w_up_f, a0_f, a_up_f, w0_b, w_up_b, a0_b, a_up_b,
               g_up, k_k, k_a, r_k, lnx_w, lnx_b, w_out,
               ffn2_pre_g, ffn2_post_g, ffn2_w_gate, ffn2_w_up, ffn2_w_down,
               ple_pre_g, ple_post_g, ple_gate_w, ple_proj_w)
    y_prompt = x_prompt
    y_sample = x_sample
    for i in range(DEPTH):
        wi = [w[i] for w in weights]
        y_prompt = _layer(y_prompt, p_prompt[i], *wi)
        y_sample = _layer(y_sample, p_sample[i], *wi)
    return (y_prompt, y_sample)
```

```python
import functools
import math

import jax
import jax.numpy as jnp
from jax import lax
from jax.experimental import pallas as pl
from jax.experimental.pallas import tpu as pltpu

F32 = jnp.float32
BF16 = jnp.bfloat16

D_MODEL = 2048
PLE_DIM = 256
POOL_WIDTH = 1024
POOL_WINDOWS = (2, 4, 8, 16)
POOL_GROUP = 256
RWKV_WIDTH = 1024
HEAD_SIZE = 64
DECAY_LORA = 64
ICL_LORA = 64
GATE_LORA = 160
FFN_DIM = 5632
RMS_EPS = 1e-6
LNX_EPS = 64e-5
DECAY_SCALE = math.exp(-0.5)

LANES = 128
SUBLANES = 8
N_PAIRS = RWKV_WIDTH // LANES
LORA_PAD = 384
Z_COLS = POOL_WIDTH + 3 * RWKV_WIDTH + LORA_PAD
CHUNK = 64
HALO = SUBLANES
VMEM_LIMIT = 56 * 1024 * 1024

FFN_TM, FFN_TF = 512, 512
INP_TM, INP_TN = 512, 896
PREP_T = 256
SCAN_T = 256
POST_TM = 256
PLE_TM = 512


def _dot(a, b):
    return jnp.dot(a, b, preferred_element_type=F32)


def _dot_nt(a, b):
    return lax.dot_general(a, b, (((1,), (1,)), ((), ())), preferred_element_type=F32)


def _dot_tn(a, b):
    return lax.dot_general(a, b, (((0,), (0,)), ((), ())), preferred_element_type=F32)


def _split3(x):
    h = x.astype(BF16)
    r1 = x - h.astype(F32)
    m = r1.astype(BF16)
    l = (r1 - m.astype(F32)).astype(BF16)
    return h, m, l


def _dot3_lhs(x, w):
    h, m, l = _split3(x)
    return _dot(h, w) + _dot(m, w) + _dot(l, w)


def _dot3_rhs(w, x):
    h, m, l = _split3(x)
    return _dot(w, h) + _dot(w, m) + _dot(w, l)


def _sigmoid(x):
    return 1.0 / (1.0 + jnp.exp(-x))


def _rms(x, g):
    ms = jnp.mean(x * x, axis=-1, keepdims=True)
    return x * lax.rsqrt(ms + RMS_EPS) * g


def _head_ones():
    r = lax.broadcasted_iota(jnp.int32, (LANES, LANES), 0) // HEAD_SIZE
    c = lax.broadcasted_iota(jnp.int32, (LANES, LANES), 1) // HEAD_SIZE
    return jnp.where(r == c, 1.0, 0.0).astype(BF16)


def _params(sem):
    return pltpu.CompilerParams(dimension_semantics=sem, vmem_limit_bytes=VMEM_LIMIT)


def _ffn_kernel(x_ref, pre_g_ref, post_g_ref, wg_ref, wu_ref, wd_ref, o_ref, u_ref, acc_ref, *, n_f):
    j = pl.program_id(1)

    @pl.when(j == 0)
    def _():
        u_ref[...] = _rms(x_ref[...], pre_g_ref[...]).astype(BF16)
        acc_ref[...] = jnp.zeros_like(acc_ref)

    u = u_ref[...]
    gate = _dot(u, wg_ref[...])
    up = _dot(u, wu_ref[...])
    h = (gate * _sigmoid(gate)) * up
    acc_ref[...] += _dot(h.astype(BF16), wd_ref[...])

    @pl.when(j == n_f - 1)
    def _():
        o_ref[...] = x_ref[...] + 0.5 * _rms(acc_ref[...], post_g_ref[...])


def _ffn(x, pre_g, post_g, wg, wu, wd):
    n = x.shape[0]
    tm, tf = min(FFN_TM, n), FFN_TF
    grid = (n // tm, FFN_DIM // tf)
    row = pl.BlockSpec((tm, D_MODEL), lambda i, j: (i, 0))
    vec = pl.BlockSpec((1, D_MODEL), lambda i, j: (0, 0))
    return pl.pallas_call(
        functools.partial(_ffn_kernel, n_f=grid[1]),
        grid=grid,
        in_specs=[row, vec, vec,
                  pl.BlockSpec((D_MODEL, tf), lambda i, j: (0, j)),
                  pl.BlockSpec((D_MODEL, tf), lambda i, j: (0, j)),
                  pl.BlockSpec((tf, D_MODEL), lambda i, j: (j, 0))],
        out_specs=row,
        out_shape=jax.ShapeDtypeStruct((n, D_MODEL), F32),
        scratch_shapes=[pltpu.VMEM((tm, D_MODEL), BF16), pltpu.VMEM((tm, D_MODEL), F32)],
        compiler_params=_params(("parallel", "arbitrary")),
        name="ffn",
    )(x, pre_g, post_g, wg, wu, wd)


def _inproj_kernel(x_ref, g_ref, w_ref, o_ref, u_ref):
    @pl.when(pl.program_id(1) == 0)
    def _():
        u_ref[...] = _rms(x_ref[...], g_ref[...]).astype(BF16)

    o_ref[...] = _dot(u_ref[...], w_ref[...])


def _inproj(x, g, w):
    n = x.shape[0]
    tm, tn = min(INP_TM, n), INP_TN
    grid = (n // tm, Z_COLS // tn)
    return pl.pallas_call(
        _inproj_kernel,
        grid=grid,
        in_specs=[pl.BlockSpec((tm, D_MODEL), lambda i, j: (i, 0)),
                  pl.BlockSpec((1, D_MODEL), lambda i, j: (0, 0)),
                  pl.BlockSpec((D_MODEL, tn), lambda i, j: (0, j))],
        out_specs=pl.BlockSpec((tm, tn), lambda i, j: (i, j)),
        out_shape=jax.ShapeDtypeStruct((n, Z_COLS), F32),
        scratch_shapes=[pltpu.VMEM((tm, D_MODEL), BF16)],
        compiler_params=_params(("parallel", "arbitrary")),
        name="inproj",
    )(x, g, w)


def _prep_kernel(zc_ref, zp_ref, zn_ref, mu_ref, poolw_ref, pscale_ref,
                 wup_f_ref, aup_f_ref, wup_b_ref, aup_b_ref, gup_ref,
                 w0f_ref, a0f_ref, w0b_ref, a0b_ref, kk_ref, ka_ref, rk_ref,
                 pool_o, v_o, bonus_o, g_o,
                 at_f, rt_f, bt_f, kt_f, bp_f, kp_f, pe_f,
                 at_b, rt_b, bt_b, kt_b, bp_b, kp_b, pe_b,
                 ext_ref, *, seq_len, n_tiles):
    t_tile = zc_ref.shape[0]
    n_chunks = t_tile // CHUNK
    i = pl.program_id(1)

    ext_ref[0:HALO, :] = jnp.where(i > 0, zp_ref[...], 0.0)
    ext_ref[HALO:HALO + t_tile, :] = zc_ref[...]
    ext_ref[HALO + t_tile:, :] = jnp.where(i < n_tiles - 1, zn_ref[...], 0.0)

    def rows(d, col, width):
        return ext_ref[HALO + d:HALO + d + t_tile, col:col + width]

    pos = i * t_tile + lax.broadcasted_iota(jnp.int32, (t_tile, 1), 0)
    for gi, win in enumerate(POOL_WINDOWS):
        half = win // 2
        col = gi * POOL_GROUP
        tot = rows(-half, col, POOL_GROUP)
        for d in range(-half + 1, win - half):
            tot = tot + rows(d, col, POOL_GROUP)
        cnt = (jnp.minimum(pos + (win - half), seq_len) - jnp.maximum(pos - half, 0)).astype(F32)
        pooled = tot / cnt - rows(0, col, POOL_GROUP)
        mixed = _dot(pooled.astype(BF16), poolw_ref[gi])
        pool_o[:, col:col + POOL_GROUP] = (mixed * pscale_ref[:, col:col + POOL_GROUP]).astype(BF16)

    def shifted(col, width):
        zc = rows(0, col, width)
        nb = 0.5 * (rows(-1, col, width) + rows(1, col, width))
        return zc + mu_ref[:, col - POOL_WIDTH:col - POOL_WIDTH + width] * (nb - zc)

    lora = shifted(POOL_WIDTH + 3 * RWKV_WIDTH, LORA_PAD)
    dwa = lora[:, 0:LANES]
    tw = jnp.tanh(dwa).astype(BF16)
    da = dwa.astype(BF16)
    sg = _sigmoid(lora[:, LANES:]).astype(BF16)

    rr = lax.broadcasted_iota(jnp.int32, (t_tile, t_tile), 0)
    cc = lax.broadcasted_iota(jnp.int32, (t_tile, t_tile), 1)
    same = (rr // CHUNK) == (cc // CHUNK)
    m_fwd = jnp.where(same & (cc <= rr), 1.0, 0.0).astype(BF16)
    m_bwd = jnp.where(same & (cc >= rr), 1.0, 0.0).astype(BF16)
    m_all = jnp.where(same, 1.0, 0.0).astype(BF16)
    ones_bd = _head_ones()

    dirs = ((wup_f_ref, aup_f_ref, w0f_ref, a0f_ref, m_fwd, (at_f, rt_f, bt_f, kt_f, bp_f, kp_f, pe_f)),
            (wup_b_ref, aup_b_ref, w0b_ref, a0b_ref, m_bwd, (at_b, rt_b, bt_b, kt_b, bp_b, kp_b, pe_b)))

    for p in range(N_PAIRS):
        ls = slice(p * LANES, (p + 1) * LANES)
        r = shifted(POOL_WIDTH + p * LANES, LANES)
        k = shifted(POOL_WIDTH + RWKV_WIDTH + p * LANES, LANES)
        v = shifted(POOL_WIDTH + 2 * RWKV_WIDTH + p * LANES, LANES)
        v_o[:, ls] = v.astype(BF16)
        g_o[:, ls] = _dot(sg, gup_ref[:, ls])

        kkr = k * kk_ref[:, ls]
        nrm = jnp.sqrt(_dot3_lhs(kkr * kkr, ones_bd))
        kk = kkr / jnp.maximum(nrm, 1e-12)

        ksum = None
        for wup_ref, aup_ref, w0_ref, a0_ref, m_dir, outs in dirs:
            o_at, o_rt, o_bt, o_kt, o_bp, o_kp, o_pe = outs
            xw = w0_ref[:, ls] + _dot(tw, wup_ref[:, ls])
            logw = -DECAY_SCALE * _sigmoid(xw)
            a = _sigmoid(a0_ref[:, ls] + _dot(da, aup_ref[:, ls]))
            kd = k * (1.0 + (a - 1.0) * ka_ref[:, ls])
            ksum = kd if ksum is None else ksum + kd
            beta = kk * a
            lh, lm, ll = _split3(logw)
            lp = _dot(m_dir, lh) + _dot(m_dir, lm) + _dot(m_dir, ll)
            lpe = _dot(m_all, lh) + _dot(m_all, lm) + _dot(m_all, ll)
            e_prev = jnp.exp(lp - logw)
            e_inc = jnp.exp(lp)
            e_inv = jnp.exp(-lp)
            e_end = jnp.exp(lpe - lp)
            o_at[:, ls] = (-(e_prev * kk)).astype(BF16)
            o_rt[:, ls] = (e_inc * r).astype(BF16)
            o_bt[:, ls] = (e_inv * beta).astype(BF16)
            o_kt[:, ls] = (e_inv * kd).astype(BF16)
            o_bp[:, ls] = (e_end * beta).astype(BF16)
            o_kp[:, ls] = (e_end * kd).astype(BF16)
            pend = jnp.exp(lpe)
            for c in range(n_chunks):
                o_pe[c, :, ls] = pend[c * CHUNK:c * CHUNK + 1, :]

        bonus_o[:, ls] = _dot3_lhs(r * ksum * rk_ref[:, ls], ones_bd) * v


def _prep(z, wts, seq_len):
    b = z.shape[0]
    t = min(PREP_T, seq_len)
    n_tiles = seq_len // t
    n_halo = t // HALO
    last_halo = seq_len // HALO - 1
    cur = pl.BlockSpec((None, t, Z_COLS), lambda bi, i: (bi, i, 0))
    prev = pl.BlockSpec((None, HALO, Z_COLS), lambda bi, i: (bi, jnp.maximum(i * n_halo - 1, 0), 0))
    nxt = pl.BlockSpec((None, HALO, Z_COLS), lambda bi, i: (bi, jnp.minimum((i + 1) * n_halo, last_halo), 0))

    def const(shape):
        nd = len(shape)
        return pl.BlockSpec(shape, lambda bi, i: (0,) * nd)

    tok = pl.BlockSpec((None, t, RWKV_WIDTH), lambda bi, i: (bi, i, 0))
    pe = pl.BlockSpec((None, t // CHUNK, 1, RWKV_WIDTH), lambda bi, i: (bi, i, 0, 0))
    tok_bf = jax.ShapeDtypeStruct((b, seq_len, RWKV_WIDTH), BF16)
    tok_f32 = jax.ShapeDtypeStruct((b, seq_len, RWKV_WIDTH), F32)
    pe_shape = jax.ShapeDtypeStruct((b, seq_len // CHUNK, 1, RWKV_WIDTH), F32)
    vec = const((1, RWKV_WIDTH))
    lora_w = const((LANES, RWKV_WIDTH))
    in_specs = [cur, prev, nxt,
                const((1, Z_COLS - POOL_WIDTH)), const((4, POOL_GROUP, POOL_GROUP)), const((1, POOL_WIDTH)),
                lora_w, lora_w, lora_w, lora_w, const((LORA_PAD - LANES, RWKV_WIDTH)),
                vec, vec, vec, vec, vec, vec, vec]
    out_specs = [pl.BlockSpec((None, t, POOL_WIDTH), lambda bi, i: (bi, i, 0)), tok, tok, tok] + 2 * ([tok] * 6 + [pe])
    out_shape = [jax.ShapeDtypeStruct((b, seq_len, POOL_WIDTH), BF16), tok_bf, tok_f32, tok_f32] \
        + 2 * ([tok_bf] * 6 + [pe_shape])
    return pl.pallas_call(
        functools.partial(_prep_kernel, seq_len=seq_len, n_tiles=n_tiles),
        grid=(b, n_tiles),
        in_specs=in_specs,
        out_specs=out_specs,
        out_shape=out_shape,
        scratch_shapes=[pltpu.VMEM((t + 2 * HALO, Z_COLS), F32)],
        compiler_params=_params(("parallel", "arbitrary")),
        name="prep",
    )(z, z, z, *wts)


def _scan_kernel(v_f, at_f, rt_f, bt_f, kt_f, bp_f, kp_f, pe_f,
                 v_b, at_b, rt_b, bt_b, kt_b, bp_b, kp_b, pe_b,
                 y_f, y_b, s_ref):
    n_chunks = v_f.shape[0] // CHUNK

    @pl.when(pl.program_id(1) == 0)
    def _():
        s_ref[...] = jnp.zeros_like(s_ref)

    lane = lax.broadcasted_iota(jnp.int32, (CHUNK, LANES), 1)
    head0 = lane < HEAD_SIZE
    tt = lax.broadcasted_iota(jnp.int32, (CHUNK, LANES), 0)
    ss = lane % HEAD_SIZE
    r2 = lax.broadcasted_iota(jnp.int32, (LANES, LANES), 0)
    c2 = lax.broadcasted_iota(jnp.int32, (LANES, LANES), 1)
    bd = (r2 // HEAD_SIZE) == (c2 // HEAD_SIZE)
    eye = jnp.where(r2 == c2, 1.0, 0.0).astype(F32)
    n_double = int(math.log2(CHUNK)) - 1

    def stack2(x):
        zero = jnp.zeros_like(x)
        return jnp.concatenate([jnp.where(head0, x, zero), jnp.where(head0, zero, x)], axis=0)

    def chunk_step(refs, y_ref, d, c, strict, incl):
        v_r, at_r, rt_r, bt_r, kt_r, bp_r, kp_r, pe_r = refs
        row0 = pl.multiple_of(c * CHUNK, CHUNK)
        rs = pl.ds(row0, CHUNK)
        pend_all = pe_r[c]
        for p in range(N_PAIRS):
            ls = slice(p * LANES, (p + 1) * LANES)
            at, rt, bt, kt = at_r[rs, ls], rt_r[rs, ls], bt_r[rs, ls], kt_r[rs, ls]
            bp, kp, v = bp_r[rs, ls], kp_r[rs, ls], v_r[rs, ls]
            s = s_ref[d, p]
            sb = s.astype(BF16)

            hh = _dot_nt(jnp.concatenate([at, rt], axis=0),
                         jnp.concatenate([stack2(bt), stack2(kt)], axis=0))
            h_ab = jnp.where(strict, hh[0:CHUNK, 0:LANES], 0.0)
            h_ak = jnp.where(strict, hh[0:CHUNK, LANES:], 0.0)
            h_rb = jnp.where(incl, hh[CHUNK:, 0:LANES], 0.0)
            h_rk = jnp.where(incl, hh[CHUNK:, LANES:], 0.0)

            a_blk = jnp.where(bd, jnp.concatenate([h_ab, h_ab], axis=0), 0.0)
            x = a_blk.astype(BF16)
            t_inv = eye + a_blk
            for _ in range(n_double):
                x2 = _dot(x, x)
                x = x2.astype(BF16)
                t_inv = t_inv + _dot(t_inv.astype(BF16), x)
            t_h = (t_inv[0:CHUNK] + t_inv[CHUNK:]).astype(BF16)

            vs = stack2(v)
            w = _dot_nt(at, sb) + _dot(h_ak.astype(BF16), vs)
            u = _dot(t_h, stack2(w.astype(BF16)))
            ub = u.astype(BF16)
            y = _dot_nt(rt, sb) + _dot(h_rb.astype(BF16), stack2(ub)) + _dot(h_rk.astype(BF16), vs)
            y_ref[rs, ls] = y

            upd = _dot_tn(jnp.concatenate([ub, v], axis=0), jnp.concatenate([bp, kp], axis=0))
            s_ref[d, p] = s * pend_all[:, ls] + jnp.where(bd, upd, 0.0)

    fwd_refs = (v_f, at_f, rt_f, bt_f, kt_f, bp_f, kp_f, pe_f)
    bwd_refs = (v_b, at_b, rt_b, bt_b, kt_b, bp_b, kp_b, pe_b)

    def body(ci, carry):
        chunk_step(fwd_refs, y_f, 0, ci, ss < tt, ss <= tt)
        chunk_step(bwd_refs, y_b, 1, n_chunks - 1 - ci, ss > tt, ss >= tt)
        return carry

    lax.fori_loop(0, n_chunks, body, 0)


def _scan(v, fwd, bwd, seq_len):
    b = v.shape[0]
    t = min(SCAN_T, seq_len)
    n_tiles = seq_len // t
    nc = t // CHUNK

    def tok(rev):
        if rev:
            return pl.BlockSpec((None, t, RWKV_WIDTH), lambda bi, i: (bi, n_tiles - 1 - i, 0))
        return pl.BlockSpec((None, t, RWKV_WIDTH), lambda bi, i: (bi, i, 0))

    def pe(rev):
        if rev:
            return pl.BlockSpec((None, nc, 1, RWKV_WIDTH), lambda bi, i: (bi, n_tiles - 1 - i, 0, 0))
        return pl.BlockSpec((None, nc, 1, RWKV_WIDTH), lambda bi, i: (bi, i, 0, 0))

    in_specs = [tok(False)] * 7 + [pe(False)] + [tok(True)] * 7 + [pe(True)]
    y_shape = jax.ShapeDtypeStruct((b, seq_len, RWKV_WIDTH), F32)
    return pl.pallas_call(
        _scan_kernel,
        grid=(b, n_tiles),
        in_specs=in_specs,
        out_specs=[tok(False), tok(True)],
        out_shape=[y_shape, y_shape],
        scratch_shapes=[pltpu.VMEM((2, N_PAIRS, LANES, LANES), F32)],
        compiler_params=_params(("parallel", "arbitrary")),
        name="scan",
    )(v, *fwd, v, *bwd)


def _post_kernel(x_ref, yf_ref, yb_ref, bonus_ref, g_ref, pool_ref, lnw_ref, lnb_ref,
                 wout_ref, postg_ref, o_ref, mix_ref):
    ones_bd = _head_ones()
    mix_ref[:, 0:POOL_WIDTH] = pool_ref[...]
    inv_n = 1.0 / HEAD_SIZE
    for p in range(N_PAIRS):
        ls = slice(p * LANES, (p + 1) * LANES)
        y = yf_ref[:, ls] + yb_ref[:, ls]
        mean = _dot3_lhs(y, ones_bd) * inv_n
        yc = y - mean
        var = _dot3_lhs(yc * yc, ones_bd) * inv_n
        yn = yc * lax.rsqrt(var + LNX_EPS) * lnw_ref[:, ls] + lnb_ref[:, ls]
        out = (yn + bonus_ref[:, ls]) * g_ref[:, ls]
        mix_ref[:, POOL_WIDTH + p * LANES:POOL_WIDTH + (p + 1) * LANES] = out.astype(BF16)
    mixed = _dot(mix_ref[...], wout_ref[...])
    o_ref[...] = x_ref[...] + _rms(mixed, postg_ref[...])


def _post(x, yf, yb, bonus, g, pool, lnw, lnb, wout, postg):
    n = x.shape[0]
    tm = min(POST_TM, n)
    row = pl.BlockSpec((tm, D_MODEL), lambda i: (i, 0))
    half = pl.BlockSpec((tm, RWKV_WIDTH), lambda i: (i, 0))
    vec_h = pl.BlockSpec((1, RWKV_WIDTH), lambda i: (0, 0))
    return pl.pallas_call(
        _post_kernel,
        grid=(n // tm,),
        in_specs=[row, half, half, half, half, half, vec_h, vec_h,
                  pl.BlockSpec((D_MODEL, D_MODEL), lambda i: (0, 0)),
                  pl.BlockSpec((1, D_MODEL), lambda i: (0, 0))],
        out_specs=row,
        out_shape=jax.ShapeDtypeStruct((n, D_MODEL), F32),
        scratch_shapes=[pltpu.VMEM((tm, D_MODEL), BF16)],
        compiler_params=_params(("parallel",)),
        name="post",
    )(x, yf, yb, bonus, g, pool, lnw, lnb, wout, postg)


def _ple_kernel(x_ref, p_ref, preg_ref, postg_ref, wgate_ref, wproj_ref, o_ref):
    x = x_ref[...]
    gate = _sigmoid(_dot(_rms(x, preg_ref[...]).astype(BF16), wgate_ref[...]))
    proj = _dot(p_ref[...].astype(BF16), wproj_ref[...])
    o_ref[...] = x + _rms(gate * proj, postg_ref[...])


def _ple(x, p, preg, postg, wgate, wproj):
    n = x.shape[0]
    tm = min(PLE_TM, n)
    row = pl.BlockSpec((tm, D_MODEL), lambda i: (i, 0))
    vec = pl.BlockSpec((1, D_MODEL), lambda i: (0, 0))
    return pl.pallas_call(
        _ple_kernel,
        grid=(n // tm,),
        in_specs=[row, pl.BlockSpec((tm, PLE_DIM), lambda i: (i, 0)), vec, vec,
                  pl.BlockSpec((D_MODEL, D_MODEL), lambda i: (0, 0)),
                  pl.BlockSpec((PLE_DIM, D_MODEL), lambda i: (0, 0))],
        out_specs=row,
        out_shape=jax.ShapeDtypeStruct((n, D_MODEL), F32),
        compiler_params=_params(("parallel",)),
        name="ple",
    )(x, p, preg, postg, wgate, wproj)


def _pad_rows(w, before, total):
    return jnp.pad(w, ((before, total - before - w.shape[0]), (0, 0)))


def _prepare_weights(ffn1_pre_g, ffn1_post_g, ffn1_w_gate, ffn1_w_up, ffn1_w_down,
                     mix_pre_g, mix_post_g, w_in, mu_shift, pool_w, pool_scale,
                     w0_f, w_up_f, a0_f, a_up_f, w0_b, w_up_b, a0_b, a_up_b,
                     g_up, k_k, k_a, r_k, lnx_w, lnx_b, w_out,
                     ffn2_pre_g, ffn2_post_g, ffn2_w_gate, ffn2_w_up, ffn2_w_down,
                     ple_pre_g, ple_post_g, ple_gate_w, ple_proj_w):
    row = lambda a: a.reshape(1, -1)
    bf = lambda a: a.astype(BF16)
    pad_c = Z_COLS - w_in.shape[1]
    prep = (row(jnp.pad(mu_shift, (0, pad_c))), bf(pool_w), row(pool_scale),
            bf(_pad_rows(w_up_f, 0, LANES)), bf(_pad_rows(a_up_f, DECAY_LORA, LANES)),
            bf(_pad_rows(w_up_b, 0, LANES)), bf(_pad_rows(a_up_b, DECAY_LORA, LANES)),
            bf(_pad_rows(g_up, 0, LORA_PAD - LANES)),
            row(w0_f), row(a0_f), row(w0_b), row(a0_b), row(k_k), row(k_a), row(r_k))
    return dict(
        ffn1=(row(ffn1_pre_g), row(ffn1_post_g), bf(ffn1_w_gate), bf(ffn1_w_up), bf(ffn1_w_down)),
        inproj=(row(mix_pre_g), bf(jnp.pad(w_in, ((0, 0), (0, pad_c))))),
        prep=prep,
        post=(row(lnx_w), row(lnx_b), bf(w_out), row(mix_post_g)),
        ffn2=(row(ffn2_pre_g), row(ffn2_post_g), bf(ffn2_w_gate), bf(ffn2_w_up), bf(ffn2_w_down)),
        ple=(row(ple_pre_g), row(ple_post_g), bf(ple_gate_w), bf(ple_proj_w)),
    )


def _layer(x, p, wts):
    b, seq_len, _ = x.shape
    n = b * seq_len
    x = _ffn(x.reshape(n, D_MODEL), *wts["ffn1"])
    z = _inproj(x, *wts["inproj"]).reshape(b, seq_len, Z_COLS)
    outs = _prep(z, wts["prep"], seq_len)
    pool, v, bonus, g = outs[:4]
    fwd, bwd = outs[4:11], outs[11:18]
    yf, yb = _scan(v, fwd, bwd, seq_len)
    flat = lambda a: a.reshape(n, a.shape[-1])
    x = _post(x, flat(yf), flat(yb), flat(bonus), flat(g), flat(pool), *wts["post"])
    x = _ffn(x, *wts["ffn2"])
    x = _ple(x, p.reshape(n, PLE_DIM), *wts["ple"])
    return x.reshape(b, seq_len, D_MODEL)


def kernel(x_prompt, x_sample, p_prompt, p_sample, ffn1_pre_g, ffn1_post_g, ffn1_w_gate, ffn1_w_up, ffn1_w_down, mix_pre_g, mix_post_g, w_in, mu_shift, pool_w, pool_scale, w0_f, w_up_f, a0_f, a_up_f, w0_b, w_up_b, a0_b, a_up_b, g_up, k_k, k_a, r_k, lnx_w, lnx_b, w_out, ffn2_pre_g, ffn2_post_g, ffn2_w_gate, ffn2_w_up, ffn2_w_down, ple_pre_g, ple_post_g, ple_gate_w, ple_proj_w):
    weights = (ffn1_pre_g, ffn1_post_g, ffn1_w_gate, ffn1_w_up, ffn1_w_down, mix_pre_g, mix_post_g, w_in,
               mu_shift, pool_w, pool_scale, w0_f, w_up_f, a0_f, a_up_f, w0_b, w_up_b, a0_b, a_up_b,
               g_up, k_k, k_a, r_k, lnx_w, lnx_b, w_out, ffn2_pre_g, ffn2_post_g, ffn2_w_gate, ffn2_w_up,
               ffn2_w_down, ple_pre_g, ple_post_g, ple_gate_w, ple_proj_w)
    y_prompt, y_sample = x_prompt, x_sample
    for layer in range(ffn1_pre_g.shape[0]):
        wts = _prepare_weights(*(w[layer] for w in weights))
        y_prompt = _layer(y_prompt, p_prompt[layer], wts)
        y_sample = _layer(y_sample, p_sample[layer], wts)
    return (y_prompt, y_sample)
```

```python
import functools
import math

import jax
import jax.numpy as jnp
from jax import lax
from jax.experimental import pallas as pl
from jax.experimental.pallas import tpu as pltpu

F32 = jnp.float32
BF16 = jnp.bfloat16

D_MODEL = 2048
PLE_DIM = 256
POOL_WIDTH = 1024
POOL_WINDOWS = (2, 4, 8, 16)
POOL_GROUP = 256
RWKV_WIDTH = 1024
HEAD_SIZE = 64
DECAY_LORA = 64
ICL_LORA = 64
GATE_LORA = 160
FFN_DIM = 5632
RMS_EPS = 1e-6
LNX_EPS = 64e-5
DECAY_SCALE = math.exp(-0.5)

LANES = 128
SUBLANES = 8
N_PAIRS = RWKV_WIDTH // LANES
LORA_PAD = 384
Z_COLS = POOL_WIDTH + 3 * RWKV_WIDTH + LORA_PAD
CHUNK = 64
HALO = SUBLANES
VMEM_LIMIT = 56 * 1024 * 1024

FFN_TM, FFN_TF = 512, 512
INP_TM, INP_TN = 512, 896
PREP_T = 256
SCAN_T = 256
POST_TM = 256
PLE_TM = 512


def _dot(a, b):
    return jnp.dot(a, b, preferred_element_type=F32)


def _dot_nt(a, b):
    return lax.dot_general(a, b, (((1,), (1,)), ((), ())), preferred_element_type=F32)


def _dot_tn(a, b):
    return lax.dot_general(a, b, (((0,), (0,)), ((), ())), preferred_element_type=F32)


def _split3(x):
    h = x.astype(BF16)
    r1 = x - h.astype(F32)
    m = r1.astype(BF16)
    l = (r1 - m.astype(F32)).astype(BF16)
    return h, m, l


def _dot3_lhs(x, w):
    h, m, l = _split3(x)
    return _dot(h, w) + _dot(m, w) + _dot(l, w)


def _dot3_rhs(w, x):
    h, m, l = _split3(x)
    return _dot(w, h) + _dot(w, m) + _dot(w, l)


def _sigmoid(x):
    return 1.0 / (1.0 + jnp.exp(-x))


def _rms(x, g):
    ms = jnp.mean(x * x, axis=-1, keepdims=True)
    return x * lax.rsqrt(ms + RMS_EPS) * g


def _head_ones():
    r = lax.broadcasted_iota(jnp.int32, (LANES, LANES), 0) // HEAD_SIZE
    c = lax.broadcasted_iota(jnp.int32, (LANES, LANES), 1) // HEAD_SIZE
    return jnp.where(r == c, 1.0, 0.0).astype(BF16)


def _params(sem):
    return pltpu.CompilerParams(dimension_semantics=sem, vmem_limit_bytes=VMEM_LIMIT)


def _ffn_kernel(x_ref, pre_g_ref, post_g_ref, wg_ref, wu_ref, wd_ref, o_ref, u_ref, acc_ref, *, n_f):
    j = pl.program_id(1)

    @pl.when(j == 0)
    def _():
        u_ref[...] = _rms(x_ref[...], pre_g_ref[...]).astype(BF16)
        acc_ref[...] = jnp.zeros_like(acc_ref)

    u = u_ref[...]
    gate = _dot(u, wg_ref[...])
    up = _dot(u, wu_ref[...])
    h = (gate * _sigmoid(gate)) * up
    acc_ref[...] += _dot(h.astype(BF16), wd_ref[...])

    @pl.when(j == n_f - 1)
    def _():
        o_ref[...] = x_ref[...] + 0.5 * _rms(acc_ref[...], post_g_ref[...])


def _ffn(x, pre_g, post_g, wg, wu, wd):
    n = x.shape[0]
    tm, tf = min(FFN_TM, n), FFN_TF
    grid = (n // tm, FFN_DIM // tf)
    row = pl.BlockSpec((tm, D_MODEL), lambda i, j: (i, 0))
    vec = pl.BlockSpec((1, D_MODEL), lambda i, j: (0, 0))
    return pl.pallas_call(
        functools.partial(_ffn_kernel, n_f=grid[1]),
        grid=grid,
        in_specs=[row, vec, vec,
                  pl.BlockSpec((D_MODEL, tf), lambda i, j: (0, j)),
                  pl.BlockSpec((D_MODEL, tf), lambda i, j: (0, j)),
                  pl.BlockSpec((tf, D_MODEL), lambda i, j: (j, 0))],
        out_specs=row,
        out_shape=jax.ShapeDtypeStruct((n, D_MODEL), F32),
        scratch_shapes=[pltpu.VMEM((tm, D_MODEL), BF16), pltpu.VMEM((tm, D_MODEL), F32)],
        compiler_params=_params(("parallel", "arbitrary")),
        name="ffn",
    )(x, pre_g, post_g, wg, wu, wd)


def _inproj_kernel(x_ref, g_ref, w_ref, o_ref, u_ref):
    @pl.when(pl.program_id(1) == 0)
    def _():
        u_ref[...] = _rms(x_ref[...], g_ref[...]).astype(BF16)

    o_ref[...] = _dot(u_ref[...], w_ref[...])


def _inproj(x, g, w):
    n = x.shape[0]
    tm, tn = min(INP_TM, n), INP_TN
    grid = (n // tm, Z_COLS // tn)
    return pl.pallas_call(
        _inproj_kernel,
        grid=grid,
        in_specs=[pl.BlockSpec((tm, D_MODEL), lambda i, j: (i, 0)),
                  pl.BlockSpec((1, D_MODEL), lambda i, j: (0, 0)),
                  pl.BlockSpec((D_MODEL, tn), lambda i, j: (0, j))],
        out_specs=pl.BlockSpec((tm, tn), lambda i, j: (i, j)),
        out_shape=jax.ShapeDtypeStruct((n, Z_COLS), F32),
        scratch_shapes=[pltpu.VMEM((tm, D_MODEL), BF16)],
        compiler_params=_params(("parallel", "arbitrary")),
        name="inproj",
    )(x, g, w)


def _prep_kernel(zc_ref, zp_ref, zn_ref, mu_ref, poolw_ref, pscale_ref,
                 wup_f_ref, aup_f_ref, wup_b_ref, aup_b_ref, gup_ref,
                 w0f_ref, a0f_ref, w0b_ref, a0b_ref, kk_ref, ka_ref, rk_ref,
                 pool_o, v_o, bonus_o, g_o,
                 at_f, rt_f, bt_f, kt_f, bp_f, kp_f, pe_f,
                 at_b, rt_b, bt_b, kt_b, bp_b, kp_b, pe_b,
                 ext_ref, *, seq_len, n_tiles):
    t_tile = zc_ref.shape[0]
    n_chunks = t_tile // CHUNK
    i = pl.program_id(1)

    ext_ref[0:HALO, :] = jnp.where(i > 0, zp_ref[...], 0.0)
    ext_ref[HALO:HALO + t_tile, :] = zc_ref[...]
    ext_ref[HALO + t_tile:, :] = jnp.where(i < n_tiles - 1, zn_ref[...], 0.0)

    def rows(d, col, width):
        return ext_ref[HALO + d:HALO + d + t_tile, col:col + width]

    pos = i * t_tile + lax.broadcasted_iota(jnp.int32, (t_tile, 1), 0)
    for gi, win in enumerate(POOL_WINDOWS):
        half = win // 2
        col = gi * POOL_GROUP
        tot = rows(-half, col, POOL_GROUP)
        for d in range(-half + 1, win - half):
            tot = tot + rows(d, col, POOL_GROUP)
        cnt = (jnp.minimum(pos + (win - half), seq_len) - jnp.maximum(pos - half, 0)).astype(F32)
        pooled = tot / cnt - rows(0, col, POOL_GROUP)
        mixed = _dot(pooled.astype(BF16), poolw_ref[gi])
        pool_o[:, col:col + POOL_GROUP] = (mixed * pscale_ref[:, col:col + POOL_GROUP]).astype(BF16)

    def shifted(col, width):
        zc = rows(0, col, width)
        nb = 0.5 * (rows(-1, col, width) + rows(1, col, width))
        return zc + mu_ref[:, col - POOL_WIDTH:col - POOL_WIDTH + width] * (nb - zc)

    lora = shifted(POOL_WIDTH + 3 * RWKV_WIDTH, LORA_PAD)
    dwa = lora[:, 0:LANES]
    tw = jnp.tanh(dwa).astype(BF16)
    da = dwa.astype(BF16)
    sg = _sigmoid(lora[:, LANES:]).astype(BF16)

    rr = lax.broadcasted_iota(jnp.int32, (t_tile, t_tile), 0)
    cc = lax.broadcasted_iota(jnp.int32, (t_tile, t_tile), 1)
    same = (rr // CHUNK) == (cc // CHUNK)
    m_fwd = jnp.where(same & (cc <= rr), 1.0, 0.0).astype(BF16)
    m_bwd = jnp.where(same & (cc >= rr), 1.0, 0.0).astype(BF16)
    m_all = jnp.where(same, 1.0, 0.0).astype(BF16)
    ones_bd = _head_ones()

    dirs = ((wup_f_ref, aup_f_ref, w0f_ref, a0f_ref, m_fwd, (at_f, rt_f, bt_f, kt_f, bp_f, kp_f, pe_f)),
            (wup_b_ref, aup_b_ref, w0b_ref, a0b_ref, m_bwd, (at_b, rt_b, bt_b, kt_b, bp_b, kp_b, pe_b)))

    for p in range(N_PAIRS):
        ls = slice(p * LANES, (p + 1) * LANES)
        r = shifted(POOL_WIDTH + p * LANES, LANES)
        k = shifted(POOL_WIDTH + RWKV_WIDTH + p * LANES, LANES)
        v = shifted(POOL_WIDTH + 2 * RWKV_WIDTH + p * LANES, LANES)
        v_o[:, ls] = v.astype(BF16)
        g_o[:, ls] = _dot(sg, gup_ref[:, ls])

        kkr = k * kk_ref[:, ls]
        nrm = jnp.sqrt(_dot3_lhs(kkr * kkr, ones_bd))
        kk = kkr / jnp.maximum(nrm, 1e-12)

        ksum = None
        for wup_ref, aup_ref, w0_ref, a0_ref, m_dir, outs in dirs:
            o_at, o_rt, o_bt, o_kt, o_bp, o_kp, o_pe = outs
            xw = w0_ref[:, ls] + _dot(tw, wup_ref[:, ls])
            logw = -DECAY_SCALE * _sigmoid(xw)
            a = _sigmoid(a0_ref[:, ls] + _dot(da, aup_ref[:, ls]))
            kd = k * (1.0 + (a - 1.0) * ka_ref[:, ls])
            ksum = kd if ksum is None else ksum + kd
            beta = kk * a
            lh, lm, ll = _split3(logw)
            lp = _dot(m_dir, lh) + _dot(m_dir, lm) + _dot(m_dir, ll)
            lpe = _dot(m_all, lh) + _dot(m_all, lm) + _dot(m_all, ll)
            e_prev = jnp.exp(lp - logw)
            e_inc = jnp.exp(lp)
            e_inv = jnp.exp(-lp)
            e_end = jnp.exp(lpe - lp)
            o_at[:, ls] = (-(e_prev * kk)).astype(BF16)
            o_rt[:, ls] = (e_inc * r).astype(BF16)
            o_bt[:, ls] = (e_inv * beta).astype(BF16)
            o_kt[:, ls] = (e_inv * kd).astype(BF16)
            o_bp[:, ls] = (e_end * beta).astype(BF16)
            o_kp[:, ls] = (e_end * kd).astype(BF16)
            pend = jnp.exp(lpe)
            for c in range(n_chunks):
                o_pe[c, :, ls] = pend[c * CHUNK:c * CHUNK + 1, :]

        bonus_o[:, ls] = _dot3_lhs(r * ksum * rk_ref[:, ls], ones_bd) * v


def _prep(z, wts, seq_len):
    b = z.shape[0]
    t = min(PREP_T, seq_len)
    n_tiles = seq_len // t
    n_halo = t // HALO
    last_halo = seq_len // HALO - 1
    cur = pl.BlockSpec((None, t, Z_COLS), lambda bi, i: (bi, i, 0))
    prev = pl.BlockSpec((None, HALO, Z_COLS), lambda bi, i: (bi, jnp.maximum(i * n_halo - 1, 0), 0))
    nxt = pl.BlockSpec((None, HALO, Z_COLS), lambda bi, i: (bi, jnp.minimum((i + 1) * n_halo, last_halo), 0))

    def const(shape):
        nd = len(shape)
        return pl.BlockSpec(shape, lambda bi, i: (0,) * nd)

    tok = pl.BlockSpec((None, t, RWKV_WIDTH), lambda bi, i: (bi, i, 0))
    pe = pl.BlockSpec((None, t // CHUNK, 1, RWKV_WIDTH), lambda bi, i: (bi, i, 0, 0))
    tok_bf = jax.ShapeDtypeStruct((b, seq_len, RWKV_WIDTH), BF16)
    tok_f32 = jax.ShapeDtypeStruct((b, seq_len, RWKV_WIDTH), F32)
    pe_shape = jax.ShapeDtypeStruct((b, seq_len // CHUNK, 1, RWKV_WIDTH), F32)
    vec = const((1, RWKV_WIDTH))
    lora_w = const((LANES, RWKV_WIDTH))
    in_specs = [cur, prev, nxt,
                const((1, Z_COLS - POOL_WIDTH)), const((4, POOL_GROUP, POOL_GROUP)), const((1, POOL_WIDTH)),
                lora_w, lora_w, lora_w, lora_w, const((LORA_PAD - LANES, RWKV_WIDTH)),
                vec, vec, vec, vec, vec, vec, vec]
    out_specs = [pl.BlockSpec((None, t, POOL_WIDTH), lambda bi, i: (bi, i, 0)), tok, tok, tok] + 2 * ([tok] * 6 + [pe])
    out_shape = [jax.ShapeDtypeStruct((b, seq_len, POOL_WIDTH), BF16), tok_bf, tok_f32, tok_f32] \
        + 2 * ([tok_bf] * 6 + [pe_shape])
    return pl.pallas_call(
        functools.partial(_prep_kernel, seq_len=seq_len, n_tiles=n_tiles),
        grid=(b, n_tiles),
        in_specs=in_specs,
        out_specs=out_specs,
        out_shape=out_shape,
        scratch_shapes=[pltpu.VMEM((t + 2 * HALO, Z_COLS), F32)],
        compiler_params=_params(("parallel", "arbitrary")),
        name="prep",
    )(z, z, z, *wts)


def _scan_kernel(v_f, at_f, rt_f, bt_f, kt_f, bp_f, kp_f, pe_f,
                 v_b, at_b, rt_b, bt_b, kt_b, bp_b, kp_b, pe_b,
                 y_f, y_b, s_ref):
    n_chunks = v_f.shape[0] // CHUNK

    @pl.when(pl.program_id(1) == 0)
    def _():
        s_ref[...] = jnp.zeros_like(s_ref)

    lane = lax.broadcasted_iota(jnp.int32, (CHUNK, LANES), 1)
    head0 = lane < HEAD_SIZE
    tt = lax.broadcasted_iota(jnp.int32, (CHUNK, LANES), 0)
    ss = lane % HEAD_SIZE
    r2 = lax.broadcasted_iota(jnp.int32, (LANES, LANES), 0)
    c2 = lax.broadcasted_iota(jnp.int32, (LANES, LANES), 1)
    bd = (r2 // HEAD_SIZE) == (c2 // HEAD_SIZE)
    eye = jnp.where(r2 == c2, 1.0, 0.0).astype(F32)
    n_double = int(math.log2(CHUNK)) - 1

    def stack2(x):
        zero = jnp.zeros_like(x)
        return jnp.concatenate([jnp.where(head0, x, zero), jnp.where(head0, zero, x)], axis=0)

    fwd_refs = (v_f, at_f, rt_f, bt_f, kt_f, bp_f, kp_f, pe_f)
    bwd_refs = (v_b, at_b, rt_b, bt_b, kt_b, bp_b, kp_b, pe_b)

    def body(ci, carry):
        chains = []
        for d, refs, y_ref, c, strict, incl in ((0, fwd_refs, y_f, ci, ss < tt, ss <= tt),
                                                (1, bwd_refs, y_b, n_chunks - 1 - ci, ss > tt, ss >= tt)):
            rs = pl.ds(pl.multiple_of(c * CHUNK, CHUNK), CHUNK)
            pend_all = refs[7][c]
            for p in range(N_PAIRS):
                ls = slice(p * LANES, (p + 1) * LANES)
                chains.append(dict(d=d, p=p, rs=rs, ls=ls, refs=refs, y_ref=y_ref,
                                   strict=strict, incl=incl, pend=pend_all[:, ls]))

        for ch in chains:
            v_r, at_r, rt_r, bt_r, kt_r, bp_r, kp_r, _ = ch["refs"]
            rs, ls = ch["rs"], ch["ls"]
            hh = _dot_nt(jnp.concatenate([at_r[rs, ls], rt_r[rs, ls]], axis=0),
                         jnp.concatenate([stack2(bt_r[rs, ls]), stack2(kt_r[rs, ls])], axis=0))
            h_ab = jnp.where(ch["strict"], hh[0:CHUNK, 0:LANES], 0.0)
            ch["h_ak"] = jnp.where(ch["strict"], hh[0:CHUNK, LANES:], 0.0).astype(BF16)
            ch["h_rb"] = jnp.where(ch["incl"], hh[CHUNK:, 0:LANES], 0.0).astype(BF16)
            ch["h_rk"] = jnp.where(ch["incl"], hh[CHUNK:, LANES:], 0.0).astype(BF16)
            a_blk = jnp.where(bd, jnp.concatenate([h_ab, h_ab], axis=0), 0.0)
            ch["x"] = a_blk.astype(BF16)
            ch["t"] = eye + a_blk

        for ch in chains:
            v_r, at_r = ch["refs"][0], ch["refs"][1]
            rs, ls = ch["rs"], ch["ls"]
            s = s_ref[ch["d"], ch["p"]]
            ch["w"] = _dot_nt(at_r[rs, ls], s.astype(BF16)) + _dot(ch["h_ak"], stack2(v_r[rs, ls]))
            ch["x_next"] = _dot(ch["x"], ch["x"]).astype(BF16)

        for k in range(n_double):
            for ch in chains:
                x = ch["x_next"]
                ch["t"] = ch["t"] + _dot(ch["t"].astype(BF16), x)
                if k + 1 < n_double:
                    ch["x_next"] = _dot(x, x).astype(BF16)

        for ch in chains:
            t_h = (ch["t"][0:CHUNK] + ch["t"][CHUNK:]).astype(BF16)
            ch["ub"] = _dot(t_h, stack2(ch["w"].astype(BF16))).astype(BF16)

        for ch in chains:
            v_r, _, rt_r, _, _, bp_r, kp_r, _ = ch["refs"]
            rs, ls = ch["rs"], ch["ls"]
            s = s_ref[ch["d"], ch["p"]]
            v = v_r[rs, ls]
            ch["y"] = (_dot_nt(rt_r[rs, ls], s.astype(BF16)) + _dot(ch["h_rb"], stack2(ch["ub"]))
                       + _dot(ch["h_rk"], stack2(v)))
            upd = _dot_tn(jnp.concatenate([ch["ub"], v], axis=0),
                          jnp.concatenate([bp_r[rs, ls], kp_r[rs, ls]], axis=0))
            ch["s_new"] = s * ch["pend"] + jnp.where(bd, upd, 0.0)

        for ch in chains:
            ch["y_ref"][ch["rs"], ch["ls"]] = ch["y"]
            s_ref[ch["d"], ch["p"]] = ch["s_new"]
        return carry

    lax.fori_loop(0, n_chunks, body, 0)


def _scan(v, fwd, bwd, seq_len):
    b = v.shape[0]
    t = min(SCAN_T, seq_len)
    n_tiles = seq_len // t
    nc = t // CHUNK

    def tok(rev):
        if rev:
            return pl.BlockSpec((None, t, RWKV_WIDTH), lambda bi, i: (bi, n_tiles - 1 - i, 0))
        return pl.BlockSpec((None, t, RWKV_WIDTH), lambda bi, i: (bi, i, 0))

    def pe(rev):
        if rev:
            return pl.BlockSpec((None, nc, 1, RWKV_WIDTH), lambda bi, i: (bi, n_tiles - 1 - i, 0, 0))
        return pl.BlockSpec((None, nc, 1, RWKV_WIDTH), lambda bi, i: (bi, i, 0, 0))

    in_specs = [tok(False)] * 7 + [pe(False)] + [tok(True)] * 7 + [pe(True)]
    y_shape = jax.ShapeDtypeStruct((b, seq_len, RWKV_WIDTH), F32)
    return pl.pallas_call(
        _scan_kernel,
        grid=(b, n_tiles),
        in_specs=in_specs,
        out_specs=[tok(False), tok(True)],
        out_shape=[y_shape, y_shape],
        scratch_shapes=[pltpu.VMEM((2, N_PAIRS, LANES, LANES), F32)],
        compiler_params=_params(("parallel", "arbitrary")),
        name="scan",
    )(v, *fwd, v, *bwd)


def _post_kernel(x_ref, yf_ref, yb_ref, bonus_ref, g_ref, pool_ref, lnw_ref, lnb_ref,
                 wout_ref, postg_ref, o_ref, mix_ref):
    ones_bd = _head_ones()
    mix_ref[:, 0:POOL_WIDTH] = pool_ref[...]
    inv_n = 1.0 / HEAD_SIZE
    for p in range(N_PAIRS):
        ls = slice(p * LANES, (p + 1) * LANES)
        y = yf_ref[:, ls] + yb_ref[:, ls]
        mean = _dot3_lhs(y, ones_bd) * inv_n
        yc = y - mean
        var = _dot3_lhs(yc * yc, ones_bd) * inv_n
        yn = yc * lax.rsqrt(var + LNX_EPS) * lnw_ref[:, ls] + lnb_ref[:, ls]
        out = (yn + bonus_ref[:, ls]) * g_ref[:, ls]
        mix_ref[:, POOL_WIDTH + p * LANES:POOL_WIDTH + (p + 1) * LANES] = out.astype(BF16)
    mixed = _dot(mix_ref[...], wout_ref[...])
    o_ref[...] = x_ref[...] + _rms(mixed, postg_ref[...])


def _post(x, yf, yb, bonus, g, pool, lnw, lnb, wout, postg):
    n = x.shape[0]
    tm = min(POST_TM, n)
    row = pl.BlockSpec((tm, D_MODEL), lambda i: (i, 0))
    half = pl.BlockSpec((tm, RWKV_WIDTH), lambda i: (i, 0))
    vec_h = pl.BlockSpec((1, RWKV_WIDTH), lambda i: (0, 0))
    return pl.pallas_call(
        _post_kernel,
        grid=(n // tm,),
        in_specs=[row, half, half, half, half, half, vec_h, vec_h,
                  pl.BlockSpec((D_MODEL, D_MODEL), lambda i: (0, 0)),
                  pl.BlockSpec((1, D_MODEL), lambda i: (0, 0))],
        out_specs=row,
        out_shape=jax.ShapeDtypeStruct((n, D_MODEL), F32),
        scratch_shapes=[pltpu.VMEM((tm, D_MODEL), BF16)],
        compiler_params=_params(("parallel",)),
        name="post",
    )(x, yf, yb, bonus, g, pool, lnw, lnb, wout, postg)


def _ple_kernel(x_ref, p_ref, preg_ref, postg_ref, wgate_ref, wproj_ref, o_ref):
    x = x_ref[...]
    gate = _sigmoid(_dot(_rms(x, preg_ref[...]).astype(BF16), wgate_ref[...]))
    proj = _dot(p_ref[...].astype(BF16), wproj_ref[...])
    o_ref[...] = x + _rms(gate * proj, postg_ref[...])


def _ple(x, p, preg, postg, wgate, wproj):
    n = x.shape[0]
    tm = min(PLE_TM, n)
    row = pl.BlockSpec((tm, D_MODEL), lambda i: (i, 0))
    vec = pl.BlockSpec((1, D_MODEL), lambda i: (0, 0))
    return pl.pallas_call(
        _ple_kernel,
        grid=(n // tm,),
        in_specs=[row, pl.BlockSpec((tm, PLE_DIM), lambda i: (i, 0)), vec, vec,
                  pl.BlockSpec((D_MODEL, D_MODEL), lambda i: (0, 0)),
                  pl.BlockSpec((PLE_DIM, D_MODEL), lambda i: (0, 0))],
        out_specs=row,
        out_shape=jax.ShapeDtypeStruct((n, D_MODEL), F32),
        compiler_params=_params(("parallel",)),
        name="ple",
    )(x, p, preg, postg, wgate, wproj)


def _pad_rows(w, before, total):
    return jnp.pad(w, ((before, total - before - w.shape[0]), (0, 0)))


def _prepare_weights(ffn1_pre_g, ffn1_post_g, ffn1_w_gate, ffn1_w_up, ffn1_w_down,
                     mix_pre_g, mix_post_g, w_in, mu_shift, pool_w, pool_scale,
                     w0_f, w_up_f, a0_f, a_up_f, w0_b, w_up_b, a0_b, a_up_b,
                     g_up, k_k, k_a, r_k, lnx_w, lnx_b, w_out,
                     ffn2_pre_g, ffn2_post_g, ffn2_w_gate, ffn2_w_up, ffn2_w_down,
                     ple_pre_g, ple_post_g, ple_gate_w, ple_proj_w):
    row = lambda a: a.reshape(1, -1)
    bf = lambda a: a.astype(BF16)
    pad_c = Z_COLS - w_in.shape[1]
    prep = (row(jnp.pad(mu_shift, (0, pad_c))), bf(pool_w), row(pool_scale),
            bf(_pad_rows(w_up_f, 0, LANES)), bf(_pad_rows(a_up_f, DECAY_LORA, LANES)),
            bf(_pad_rows(w_up_b, 0, LANES)), bf(_pad_rows(a_up_b, DECAY_LORA, LANES)),
            bf(_pad_rows(g_up, 0, LORA_PAD - LANES)),
            row(w0_f), row(a0_f), row(w0_b), row(a0_b), row(k_k), row(k_a), row(r_k))
    return dict(
        ffn1=(row(ffn1_pre_g), row(ffn1_post_g), bf(ffn1_w_gate), bf(ffn1_w_up), bf(ffn1_w_down)),
        inproj=(row(mix_pre_g), bf(jnp.pad(w_in, ((0, 0), (0, pad_c))))),
        prep=prep,
        post=(row(lnx_w), row(lnx_b), bf(w_out), row(mix_post_g)),
        ffn2=(row(ffn2_pre_g), row(ffn2_post_g), bf(ffn2_w_gate), bf(ffn2_w_up), bf(ffn2_w_down)),
        ple=(row(ple_pre_g), row(ple_post_g), bf(ple_gate_w), bf(ple_proj_w)),
    )


def _layer(x, p, wts):
    b, seq_len, _ = x.shape
    n = b * seq_len
    x = _ffn(x.reshape(n, D_MODEL), *wts["ffn1"])
    z = _inproj(x, *wts["inproj"]).reshape(b, seq_len, Z_COLS)
    outs = _prep(z, wts["prep"], seq_len)
    pool, v, bonus, g = outs[:4]
    fwd, bwd = outs[4:11], outs[11:18]
    yf, yb = _scan(v, fwd, bwd, seq_len)
    flat = lambda a: a.reshape(n, a.shape[-1])
    x = _post(x, flat(yf), flat(yb), flat(bonus), flat(g), flat(pool), *wts["post"])
    x = _ffn(x, *wts["ffn2"])
    x = _ple(x, p.reshape(n, PLE_DIM), *wts["ple"])
    return x.reshape(b, seq_len, D_MODEL)


def kernel(x_prompt, x_sample, p_prompt, p_sample, ffn1_pre_g, ffn1_post_g, ffn1_w_gate, ffn1_w_up, ffn1_w_down, mix_pre_g, mix_post_g, w_in, mu_shift, pool_w, pool_scale, w0_f, w_up_f, a0_f, a_up_f, w0_b, w_up_b, a0_b, a_up_b, g_up, k_k, k_a, r_k, lnx_w, lnx_b, w_out, ffn2_pre_g, ffn2_post_g, ffn2_w_gate, ffn2_w_up, ffn2_w_down, ple_pre_g, ple_post_g, ple_gate_w, ple_proj_w):
    weights = (ffn1_pre_g, ffn1_post_g, ffn1_w_gate, ffn1_w_up, ffn1_w_down, mix_pre_g, mix_post_g, w_in,
               mu_shift, pool_w, pool_scale, w0_f, w_up_f, a0_f, a_up_f, w0_b, w_up_b, a0_b, a_up_b,
               g_up, k_k, k_a, r_k, lnx_w, lnx_b, w_out, ffn2_pre_g, ffn2_post_g, ffn2_w_gate, ffn2_w_up,
               ffn2_w_down, ple_pre_g, ple_post_g, ple_gate_w, ple_proj_w)
    y_prompt, y_sample = x_prompt, x_sample
    for layer in range(ffn1_pre_g.shape[0]):
        wts = _prepare_weights(*(w[layer] for w in weights))
        y_prompt = _layer(y_prompt, p_prompt[layer], wts)
        y_sample = _layer(y_sample, p_sample[layer], wts)
    return (y_prompt, y_sample)
```

```python
import functools
import math

import jax
import jax.numpy as jnp
from jax import lax
from jax.experimental import pallas as pl
from jax.experimental.pallas import tpu as pltpu

F32 = jnp.float32
BF16 = jnp.bfloat16

D_MODEL = 2048
PLE_DIM = 256
POOL_WIDTH = 1024
POOL_WINDOWS = (2, 4, 8, 16)
POOL_GROUP = 256
RWKV_WIDTH = 1024
HEAD_SIZE = 64
DECAY_LORA = 64
ICL_LORA = 64
GATE_LORA = 160
FFN_DIM = 5632
RMS_EPS = 1e-6
LNX_EPS = 64e-5
LOG2E = math.log2(math.e)
DECAY_SCALE2 = math.exp(-0.5) * LOG2E

LANES = 128
SUBLANES = 8
N_PAIRS = RWKV_WIDTH // LANES
LORA_PAD = 384
Z_COLS = POOL_WIDTH + 3 * RWKV_WIDTH + LORA_PAD
CHUNK = 64
HALO = SUBLANES
VMEM_LIMIT = 56 * 1024 * 1024

FFN_TM, FFN_TF = 512, 512
INP_TM, INP_TN = 512, 896
PREP_T = 256
SCAN_T = 256
POST_TM = 256
PLE_TM = 512


def _dot(a, b):
    return jnp.dot(a, b, preferred_element_type=F32)


def _dot_nt(a, b):
    return lax.dot_general(a, b, (((1,), (1,)), ((), ())), preferred_element_type=F32)


def _dot_tn(a, b):
    return lax.dot_general(a, b, (((0,), (0,)), ((), ())), preferred_element_type=F32)


def _split2(x):
    h = x.astype(BF16)
    l = (x - h.astype(F32)).astype(BF16)
    return h, l


def _dot2_lhs(x, w):
    h, l = _split2(x)
    return _dot(h, w) + _dot(l, w)


def _sigmoid(x):
    return 1.0 / (1.0 + jnp.exp2(x * (-LOG2E)))


def _rms(x, g):
    ms = jnp.mean(x * x, axis=-1, keepdims=True)
    return x * lax.rsqrt(ms + RMS_EPS) * g


def _head_ones():
    r = lax.broadcasted_iota(jnp.int32, (LANES, LANES), 0) // HEAD_SIZE
    c = lax.broadcasted_iota(jnp.int32, (LANES, LANES), 1) // HEAD_SIZE
    return jnp.where(r == c, 1.0, 0.0).astype(BF16)


def _params(sem):
    return pltpu.CompilerParams(dimension_semantics=sem, vmem_limit_bytes=VMEM_LIMIT)


def _ffn_kernel(x_ref, pre_g_ref, post_g_ref, wg_ref, wu_ref, wd_ref, o_ref, u_ref, acc_ref, *, n_f):
    j = pl.program_id(1)

    @pl.when(j == 0)
    def _():
        u_ref[...] = _rms(x_ref[...], pre_g_ref[...]).astype(BF16)
        acc_ref[...] = jnp.zeros_like(acc_ref)

    u = u_ref[...]
    gate = _dot(u, wg_ref[...])
    up = _dot(u, wu_ref[...])
    h = (gate * _sigmoid(gate)) * up
    acc_ref[...] += _dot(h.astype(BF16), wd_ref[...])

    @pl.when(j == n_f - 1)
    def _():
        o_ref[...] = x_ref[...] + 0.5 * _rms(acc_ref[...], post_g_ref[...])


def _ffn(x, pre_g, post_g, wg, wu, wd):
    n = x.shape[0]
    tm, tf = min(FFN_TM, n), FFN_TF
    grid = (n // tm, FFN_DIM // tf)
    row = pl.BlockSpec((tm, D_MODEL), lambda i, j: (i, 0))
    vec = pl.BlockSpec((1, D_MODEL), lambda i, j: (0, 0))
    return pl.pallas_call(
        functools.partial(_ffn_kernel, n_f=grid[1]),
        grid=grid,
        in_specs=[row, vec, vec,
                  pl.BlockSpec((D_MODEL, tf), lambda i, j: (0, j)),
                  pl.BlockSpec((D_MODEL, tf), lambda i, j: (0, j)),
                  pl.BlockSpec((tf, D_MODEL), lambda i, j: (j, 0))],
        out_specs=row,
        out_shape=jax.ShapeDtypeStruct((n, D_MODEL), F32),
        scratch_shapes=[pltpu.VMEM((tm, D_MODEL), BF16), pltpu.VMEM((tm, D_MODEL), F32)],
        compiler_params=_params(("parallel", "arbitrary")),
        name="ffn",
    )(x, pre_g, post_g, wg, wu, wd)


def _inproj_kernel(x_ref, g_ref, w_ref, o_ref, u_ref):
    @pl.when(pl.program_id(1) == 0)
    def _():
        u_ref[...] = _rms(x_ref[...], g_ref[...]).astype(BF16)

    o_ref[...] = _dot(u_ref[...], w_ref[...])


def _inproj(x, g, w):
    n = x.shape[0]
    tm, tn = min(INP_TM, n), INP_TN
    grid = (n // tm, Z_COLS // tn)
    return pl.pallas_call(
        _inproj_kernel,
        grid=grid,
        in_specs=[pl.BlockSpec((tm, D_MODEL), lambda i, j: (i, 0)),
                  pl.BlockSpec((1, D_MODEL), lambda i, j: (0, 0)),
                  pl.BlockSpec((D_MODEL, tn), lambda i, j: (0, j))],
        out_specs=pl.BlockSpec((tm, tn), lambda i, j: (i, j)),
        out_shape=jax.ShapeDtypeStruct((n, Z_COLS), F32),
        scratch_shapes=[pltpu.VMEM((tm, D_MODEL), BF16)],
        compiler_params=_params(("parallel", "arbitrary")),
        name="inproj",
    )(x, g, w)


def _prep_kernel(zc_ref, zp_ref, zn_ref, mu_ref, poolw_ref, pscale_ref,
                 wup_f_ref, aup_f_ref, wup_b_ref, aup_b_ref, gup_ref,
                 w0f_ref, a0f_ref, w0b_ref, a0b_ref, kk_ref, ka_ref, rk_ref,
                 pool_o, v_o, bonus_o, g_o,
                 at_f, rt_f, bt_f, kt_f, bp_f, kp_f, pe_f,
                 at_b, rt_b, bt_b, kt_b, bp_b, kp_b, pe_b,
                 ext_ref, *, seq_len, n_tiles):
    t_tile = zc_ref.shape[0]
    n_chunks = t_tile // CHUNK
    i = pl.program_id(1)

    ext_ref[0:HALO, :] = jnp.where(i > 0, zp_ref[...], 0.0)
    ext_ref[HALO:HALO + t_tile, :] = zc_ref[...]
    ext_ref[HALO + t_tile:, :] = jnp.where(i < n_tiles - 1, zn_ref[...], 0.0)

    def rows(d, col, width):
        return ext_ref[HALO + d:HALO + d + t_tile, col:col + width]

    pos = i * t_tile + lax.broadcasted_iota(jnp.int32, (t_tile, 1), 0)
    for gi, win in enumerate(POOL_WINDOWS):
        half = win // 2
        col = gi * POOL_GROUP
        tot = rows(-half, col, POOL_GROUP)
        for d in range(-half + 1, win - half):
            tot = tot + rows(d, col, POOL_GROUP)
        cnt = (jnp.minimum(pos + (win - half), seq_len) - jnp.maximum(pos - half, 0)).astype(F32)
        pooled = tot / cnt - rows(0, col, POOL_GROUP)
        mixed = _dot(pooled.astype(BF16), poolw_ref[gi])
        pool_o[:, col:col + POOL_GROUP] = (mixed * pscale_ref[:, col:col + POOL_GROUP]).astype(BF16)

    def shifted(col, width):
        mu = mu_ref[:, col - POOL_WIDTH:col - POOL_WIDTH + width]
        return (1.0 - mu) * rows(0, col, width) + (0.5 * mu) * (rows(-1, col, width) + rows(1, col, width))

    lora = shifted(POOL_WIDTH + 3 * RWKV_WIDTH, LORA_PAD)
    dwa = lora[:, 0:LANES]
    tw = jnp.tanh(dwa).astype(BF16)
    da = dwa.astype(BF16)
    sg = _sigmoid(lora[:, LANES:]).astype(BF16)

    rr = lax.broadcasted_iota(jnp.int32, (t_tile, t_tile), 0)
    cc = lax.broadcasted_iota(jnp.int32, (t_tile, t_tile), 1)
    same = (rr // CHUNK) == (cc // CHUNK)
    m_fwd = jnp.where(same & (cc <= rr), 1.0, 0.0).astype(BF16)
    m_bwd = jnp.where(same & (cc >= rr), 1.0, 0.0).astype(BF16)
    ones_bd = _head_ones()

    dirs = ((wup_f_ref, aup_f_ref, w0f_ref, a0f_ref, m_fwd, CHUNK - 1, (at_f, rt_f, bt_f, kt_f, bp_f, kp_f, pe_f)),
            (wup_b_ref, aup_b_ref, w0b_ref, a0b_ref, m_bwd, 0, (at_b, rt_b, bt_b, kt_b, bp_b, kp_b, pe_b)))

    for p in range(N_PAIRS):
        ls = slice(p * LANES, (p + 1) * LANES)
        r = shifted(POOL_WIDTH + p * LANES, LANES)
        k = shifted(POOL_WIDTH + RWKV_WIDTH + p * LANES, LANES)
        v = shifted(POOL_WIDTH + 2 * RWKV_WIDTH + p * LANES, LANES)
        v_o[:, ls] = v.astype(BF16)
        g_o[:, ls] = _dot(sg, gup_ref[:, ls])

        kkr = k * kk_ref[:, ls]
        nrm = jnp.sqrt(_dot2_lhs(kkr * kkr, ones_bd))
        kk = kkr / jnp.maximum(nrm, 1e-12)
        nkk = -kk

        ksum = None
        for wup_ref, aup_ref, w0_ref, a0_ref, m_dir, end_row, outs in dirs:
            o_at, o_rt, o_bt, o_kt, o_bp, o_kp, o_pe = outs
            xw = w0_ref[:, ls] + _dot(tw, wup_ref[:, ls])
            lw2 = -DECAY_SCALE2 * _sigmoid(xw)
            a = _sigmoid(a0_ref[:, ls] + _dot(da, aup_ref[:, ls]))
            kd = k * (1.0 + (a - 1.0) * ka_ref[:, ls])
            ksum = kd if ksum is None else ksum + kd
            beta = kk * a
            lh, ll = _split2(lw2)
            lp = _dot(m_dir, lh) + _dot(m_dir, ll)
            ends = [lp[c * CHUNK + end_row:c * CHUNK + end_row + 1, :] for c in range(n_chunks)]
            lpe = jnp.concatenate([jnp.broadcast_to(e, (CHUNK, LANES)) for e in ends], axis=0)
            e_inv = jnp.exp2(-lp)
            e_end = jnp.exp2(lpe - lp)
            o_at[:, ls] = (jnp.exp2(lp - lw2) * nkk).astype(BF16)
            o_rt[:, ls] = (jnp.exp2(lp) * r).astype(BF16)
            o_bt[:, ls] = (e_inv * beta).astype(BF16)
            o_kt[:, ls] = (e_inv * kd).astype(BF16)
            o_bp[:, ls] = (e_end * beta).astype(BF16)
            o_kp[:, ls] = (e_end * kd).astype(BF16)
            for c in range(n_chunks):
                o_pe[c, :, ls] = jnp.exp2(ends[c])

        bonus_o[:, ls] = _dot2_lhs(r * ksum * rk_ref[:, ls], ones_bd) * v


def _prep(z, wts, seq_len):
    b = z.shape[0]
    t = min(PREP_T, seq_len)
    n_tiles = seq_len // t
    n_halo = t // HALO
    last_halo = seq_len // HALO - 1
    cur = pl.BlockSpec((None, t, Z_COLS), lambda bi, i: (bi, i, 0))
    prev = pl.BlockSpec((None, HALO, Z_COLS), lambda bi, i: (bi, jnp.maximum(i * n_halo - 1, 0), 0))
    nxt = pl.BlockSpec((None, HALO, Z_COLS), lambda bi, i: (bi, jnp.minimum((i + 1) * n_halo, last_halo), 0))

    def const(shape):
        nd = len(shape)
        return pl.BlockSpec(shape, lambda bi, i: (0,) * nd)

    tok = pl.BlockSpec((None, t, RWKV_WIDTH), lambda bi, i: (bi, i, 0))
    pe = pl.BlockSpec((None, t // CHUNK, 1, RWKV_WIDTH), lambda bi, i: (bi, i, 0, 0))
    tok_bf = jax.ShapeDtypeStruct((b, seq_len, RWKV_WIDTH), BF16)
    tok_f32 = jax.ShapeDtypeStruct((b, seq_len, RWKV_WIDTH), F32)
    pe_shape = jax.ShapeDtypeStruct((b, seq_len // CHUNK, 1, RWKV_WIDTH), F32)
    vec = const((1, RWKV_WIDTH))
    lora_w = const((LANES, RWKV_WIDTH))
    in_specs = [cur, prev, nxt,
                const((1, Z_COLS - POOL_WIDTH)), const((4, POOL_GROUP, POOL_GROUP)), const((1, POOL_WIDTH)),
                lora_w, lora_w, lora_w, lora_w, const((LORA_PAD - LANES, RWKV_WIDTH)),
                vec, vec, vec, vec, vec, vec, vec]
    out_specs = [pl.BlockSpec((None, t, POOL_WIDTH), lambda bi, i: (bi, i, 0)), tok, tok, tok] + 2 * ([tok] * 6 + [pe])
    out_shape = [jax.ShapeDtypeStruct((b, seq_len, POOL_WIDTH), BF16), tok_bf, tok_f32, tok_f32] \
        + 2 * ([tok_bf] * 6 + [pe_shape])
    return pl.pallas_call(
        functools.partial(_prep_kernel, seq_len=seq_len, n_tiles=n_tiles),
        grid=(b, n_tiles),
        in_specs=in_specs,
        out_specs=out_specs,
        out_shape=out_shape,
        scratch_shapes=[pltpu.VMEM((t + 2 * HALO, Z_COLS), F32)],
        compiler_params=_params(("parallel", "arbitrary")),
        name="prep",
    )(z, z, z, *wts)


def _scan_kernel(v_f, at_f, rt_f, bt_f, kt_f, bp_f, kp_f, pe_f,
                 v_b, at_b, rt_b, bt_b, kt_b, bp_b, kp_b, pe_b,
                 y_f, y_b, s_ref):
    n_chunks = v_f.shape[0] // CHUNK

    @pl.when(pl.program_id(1) == 0)
    def _():
        s_ref[...] = jnp.zeros_like(s_ref)

    lane = lax.broadcasted_iota(jnp.int32, (CHUNK, LANES), 1)
    head0 = lane < HEAD_SIZE
    tt = lax.broadcasted_iota(jnp.int32, (CHUNK, LANES), 0)
    ss = lane % HEAD_SIZE
    r2 = lax.broadcasted_iota(jnp.int32, (LANES, LANES), 0)
    c2 = lax.broadcasted_iota(jnp.int32, (LANES, LANES), 1)
    bd = (r2 // HEAD_SIZE) == (c2 // HEAD_SIZE)
    eye_h = jnp.where(tt == ss, 1.0, 0.0).astype(F32)
    n_double = int(math.log2(CHUNK)) - 1

    def stack2(x):
        zero = jnp.zeros_like(x)
        return jnp.concatenate([jnp.where(head0, x, zero), jnp.where(head0, zero, x)], axis=0)

    fwd_refs = (v_f, at_f, rt_f, bt_f, kt_f, bp_f, kp_f, pe_f)
    bwd_refs = (v_b, at_b, rt_b, bt_b, kt_b, bp_b, kp_b, pe_b)

    def body(ci, carry):
        chains = []
        for d, refs, y_ref, c, strict, incl in ((0, fwd_refs, y_f, ci, ss < tt, ss <= tt),
                                                (1, bwd_refs, y_b, n_chunks - 1 - ci, ss > tt, ss >= tt)):
            rs = pl.ds(pl.multiple_of(c * CHUNK, CHUNK), CHUNK)
            pend_all = refs[7][c]
            for p in range(N_PAIRS):
                ls = slice(p * LANES, (p + 1) * LANES)
                chains.append(dict(d=d, p=p, rs=rs, ls=ls, refs=refs, y_ref=y_ref,
                                   strict=strict, incl=incl, pend=pend_all[:, ls]))

        def blk(m_h):
            return jnp.where(bd, jnp.concatenate([m_h, m_h], axis=0), jnp.zeros((), m_h.dtype))

        for ch in chains:
            v_r, at_r, rt_r, bt_r, kt_r, bp_r, kp_r, _ = ch["refs"]
            rs, ls = ch["rs"], ch["ls"]
            hh = _dot_nt(jnp.concatenate([at_r[rs, ls], rt_r[rs, ls]], axis=0),
                         jnp.concatenate([stack2(bt_r[rs, ls]), stack2(kt_r[rs, ls])], axis=0))
            h_ab = jnp.where(ch["strict"], hh[0:CHUNK, 0:LANES], 0.0)
            ch["h_ak"] = jnp.where(ch["strict"], hh[0:CHUNK, LANES:], 0.0).astype(BF16)
            ch["h_rb"] = jnp.where(ch["incl"], hh[CHUNK:, 0:LANES], 0.0).astype(BF16)
            ch["h_rk"] = jnp.where(ch["incl"], hh[CHUNK:, LANES:], 0.0).astype(BF16)
            ch["x"] = h_ab.astype(BF16)
            ch["t"] = eye_h + h_ab

        for ch in chains:
            v_r, at_r = ch["refs"][0], ch["refs"][1]
            rs, ls = ch["rs"], ch["ls"]
            st_b = s_ref[ch["d"], ch["p"]].astype(BF16)
            ch["w"] = _dot(jnp.concatenate([at_r[rs, ls], ch["h_ak"]], axis=1),
                           jnp.concatenate([st_b, stack2(v_r[rs, ls])], axis=0))
            ch["x"] = _dot(ch["x"], blk(ch["x"])).astype(BF16)

        for k in range(n_double):
            for ch in chains:
                x_blk = blk(ch["x"])
                if k + 1 < n_double:
                    tx = _dot(jnp.concatenate([ch["t"].astype(BF16), ch["x"]], axis=0), x_blk)
                    ch["t"] = ch["t"] + tx[0:CHUNK]
                    ch["x"] = tx[CHUNK:].astype(BF16)
                else:
                    ch["t"] = ch["t"] + _dot(ch["t"].astype(BF16), x_blk)

        for ch in chains:
            ch["ub"] = _dot(ch["t"].astype(BF16), stack2(ch["w"].astype(BF16))).astype(BF16)

        for ch in chains:
            v_r, _, rt_r, _, _, bp_r, kp_r, _ = ch["refs"]
            rs, ls = ch["rs"], ch["ls"]
            st = s_ref[ch["d"], ch["p"]]
            v = v_r[rs, ls]
            ch["y"] = _dot(jnp.concatenate([rt_r[rs, ls], ch["h_rb"], ch["h_rk"]], axis=1),
                           jnp.concatenate([st.astype(BF16), stack2(ch["ub"]), stack2(v)], axis=0))
            upd = _dot_tn(jnp.concatenate([bp_r[rs, ls], kp_r[rs, ls]], axis=0),
                          jnp.concatenate([ch["ub"], v], axis=0))
            pend_col = jnp.broadcast_to(ch["pend"], (LANES, LANES)).T
            ch["s_new"] = st * pend_col + jnp.where(bd, upd, 0.0)

        for ch in chains:
            ch["y_ref"][ch["rs"], ch["ls"]] = ch["y"]
            s_ref[ch["d"], ch["p"]] = ch["s_new"]
        return carry

    lax.fori_loop(0, n_chunks, body, 0)


def _scan(v, fwd, bwd, seq_len):
    b = v.shape[0]
    t = min(SCAN_T, seq_len)
    n_tiles = seq_len // t
    nc = t // CHUNK

    def tok(rev):
        if rev:
            return pl.BlockSpec((None, t, RWKV_WIDTH), lambda bi, i: (bi, n_tiles - 1 - i, 0))
        return pl.BlockSpec((None, t, RWKV_WIDTH), lambda bi, i: (bi, i, 0))

    def pe(rev):
        if rev:
            return pl.BlockSpec((None, nc, 1, RWKV_WIDTH), lambda bi, i: (bi, n_tiles - 1 - i, 0, 0))
        return pl.BlockSpec((None, nc, 1, RWKV_WIDTH), lambda bi, i: (bi, i, 0, 0))

    in_specs = [tok(False)] * 7 + [pe(False)] + [tok(True)] * 7 + [pe(True)]
    y_shape = jax.ShapeDtypeStruct((b, seq_len, RWKV_WIDTH), F32)
    return pl.pallas_call(
        _scan_kernel,
        grid=(b, n_tiles),
        in_specs=in_specs,
        out_specs=[tok(False), tok(True)],
        out_shape=[y_shape, y_shape],
        scratch_shapes=[pltpu.VMEM((2, N_PAIRS, LANES, LANES), F32)],
        compiler_params=_params(("parallel", "arbitrary")),
        name="scan",
    )(v, *fwd, v, *bwd)


def _post_kernel(x_ref, yf_ref, yb_ref, bonus_ref, g_ref, pool_ref, lnw_ref, lnb_ref,
                 wout_ref, postg_ref, o_ref, mix_ref):
    ones_bd = _head_ones()
    mix_ref[:, 0:POOL_WIDTH] = pool_ref[...]
    inv_n = 1.0 / HEAD_SIZE
    for p in range(N_PAIRS):
        ls = slice(p * LANES, (p + 1) * LANES)
        y = yf_ref[:, ls] + yb_ref[:, ls]
        mean = _dot2_lhs(y, ones_bd) * inv_n
        yc = y - mean
        var = _dot2_lhs(yc * yc, ones_bd) * inv_n
        yn = yc * lax.rsqrt(var + LNX_EPS) * lnw_ref[:, ls] + lnb_ref[:, ls]
        out = (yn + bonus_ref[:, ls]) * g_ref[:, ls]
        mix_ref[:, POOL_WIDTH + p * LANES:POOL_WIDTH + (p + 1) * LANES] = out.astype(BF16)
    mixed = _dot(mix_ref[...], wout_ref[...])
    o_ref[...] = x_ref[...] + _rms(mixed, postg_ref[...])


def _post(x, yf, yb, bonus, g, pool, lnw, lnb, wout, postg):
    n = x.shape[0]
    tm = min(POST_TM, n)
    row = pl.BlockSpec((tm, D_MODEL), lambda i: (i, 0))
    half = pl.BlockSpec((tm, RWKV_WIDTH), lambda i: (i, 0))
    vec_h = pl.BlockSpec((1, RWKV_WIDTH), lambda i: (0, 0))
    return pl.pallas_call(
        _post_kernel,
        grid=(n // tm,),
        in_specs=[row, half, half, half, half, half, vec_h, vec_h,
                  pl.BlockSpec((D_MODEL, D_MODEL), lambda i: (0, 0)),
                  pl.BlockSpec((1, D_MODEL), lambda i: (0, 0))],
        out_specs=row,
        out_shape=jax.ShapeDtypeStruct((n, D_MODEL), F32),
        scratch_shapes=[pltpu.VMEM((tm, D_MODEL), BF16)],
        compiler_params=_params(("parallel",)),
        name="post",
    )(x, yf, yb, bonus, g, pool, lnw, lnb, wout, postg)


def _ple_kernel(x_ref, p_ref, preg_ref, postg_ref, wgate_ref, wproj_ref, o_ref):
    x = x_ref[...]
    gate = _sigmoid(_dot(_rms(x, preg_ref[...]).astype(BF16), wgate_ref[...]))
    proj = _dot(p_ref[...].astype(BF16), wproj_ref[...])
    o_ref[...] = x + _rms(gate * proj, postg_ref[...])


def _ple(x, p, preg, postg, wgate, wproj):
    n = x.shape[0]
    tm = min(PLE_TM, n)
    row = pl.BlockSpec((tm, D_MODEL), lambda i: (i, 0))
    vec = pl.BlockSpec((1, D_MODEL), lambda i: (0, 0))
    return pl.pallas_call(
        _ple_kernel,
        grid=(n // tm,),
        in_specs=[row, pl.BlockSpec((tm, PLE_DIM), lambda i: (i, 0)), vec, vec,
                  pl.BlockSpec((D_MODEL, D_MODEL), lambda i: (0, 0)),
                  pl.BlockSpec((PLE_DIM, D_MODEL), lambda i: (0, 0))],
        out_specs=row,
        out_shape=jax.ShapeDtypeStruct((n, D_MODEL), F32),
        compiler_params=_params(("parallel",)),
        name="ple",
    )(x, p, preg, postg, wgate, wproj)


def _pad_rows(w, before, total):
    return jnp.pad(w, ((before, total - before - w.shape[0]), (0, 0)))


def _prepare_weights(ffn1_pre_g, ffn1_post_g, ffn1_w_gate, ffn1_w_up, ffn1_w_down,
                     mix_pre_g, mix_post_g, w_in, mu_shift, pool_w, pool_scale,
                     w0_f, w_up_f, a0_f, a_up_f, w0_b, w_up_b, a0_b, a_up_b,
                     g_up, k_k, k_a, r_k, lnx_w, lnx_b, w_out,
                     ffn2_pre_g, ffn2_post_g, ffn2_w_gate, ffn2_w_up, ffn2_w_down,
                     ple_pre_g, ple_post_g, ple_gate_w, ple_proj_w):
    row = lambda a: a.reshape(1, -1)
    bf = lambda a: a.astype(BF16)
    pad_c = Z_COLS - w_in.shape[1]
    prep = (row(jnp.pad(mu_shift, (0, pad_c))), bf(pool_w), row(pool_scale),
            bf(_pad_rows(w_up_f, 0, LANES)), bf(_pad_rows(a_up_f, DECAY_LORA, LANES)),
            bf(_pad_rows(w_up_b, 0, LANES)), bf(_pad_rows(a_up_b, DECAY_LORA, LANES)),
            bf(_pad_rows(g_up, 0, LORA_PAD - LANES)),
            row(w0_f), row(a0_f), row(w0_b), row(a0_b), row(k_k), row(k_a), row(r_k))
    return dict(
        ffn1=(row(ffn1_pre_g), row(ffn1_post_g), bf(ffn1_w_gate), bf(ffn1_w_up), bf(ffn1_w_down)),
        inproj=(row(mix_pre_g), bf(jnp.pad(w_in, ((0, 0), (0, pad_c))))),
        prep=prep,
        post=(row(lnx_w), row(lnx_b), bf(w_out), row(mix_post_g)),
        ffn2=(row(ffn2_pre_g), row(ffn2_post_g), bf(ffn2_w_gate), bf(ffn2_w_up), bf(ffn2_w_down)),
        ple=(row(ple_pre_g), row(ple_post_g), bf(ple_gate_w), bf(ple_proj_w)),
    )


def _layer(x, p, wts):
    b, seq_len, _ = x.shape
    n = b * seq_len
    x = _ffn(x.reshape(n, D_MODEL), *wts["ffn1"])
    z = _inproj(x, *wts["inproj"]).reshape(b, seq_len, Z_COLS)
    outs = _prep(z, wts["prep"], seq_len)
    pool, v, bonus, g = outs[:4]
    fwd, bwd = outs[4:11], outs[11:18]
    yf, yb = _scan(v, fwd, bwd, seq_len)
    flat = lambda a: a.reshape(n, a.shape[-1])
    x = _post(x, flat(yf), flat(yb), flat(bonus), flat(g), flat(pool), *wts["post"])
    x = _ffn(x, *wts["ffn2"])
    x = _ple(x, p.reshape(n, PLE_DIM), *wts["ple"])
    return x.reshape(b, seq_len, D_MODEL)


def kernel(x_prompt, x_sample, p_prompt, p_sample, ffn1_pre_g, ffn1_post_g, ffn1_w_gate, ffn1_w_up, ffn1_w_down, mix_pre_g, mix_post_g, w_in, mu_shift, pool_w, pool_scale, w0_f, w_up_f, a0_f, a_up_f, w0_b, w_up_b, a0_b, a_up_b, g_up, k_k, k_a, r_k, lnx_w, lnx_b, w_out, ffn2_pre_g, ffn2_post_g, ffn2_w_gate, ffn2_w_up, ffn2_w_down, ple_pre_g, ple_post_g, ple_gate_w, ple_proj_w):
    weights = (ffn1_pre_g, ffn1_post_g, ffn1_w_gate, ffn1_w_up, ffn1_w_down, mix_pre_g, mix_post_g, w_in,
               mu_shift, pool_w, pool_scale, w0_f, w_up_f, a0_f, a_up_f, w0_b, w_up_b, a0_b, a_up_b,
               g_up, k_k, k_a, r_k, lnx_w, lnx_b, w_out, ffn2_pre_g, ffn2_post_g, ffn2_w_gate, ffn2_w_up,
               ffn2_w_down, ple_pre_g, ple_post_g, ple_gate_w, ple_proj_w)
    y_prompt, y_sample = x_prompt, x_sample
    for layer in range(ffn1_pre_g.shape[0]):
        wts = _prepare_weights(*(w[layer] for w in weights))
        y_prompt = _layer(y_prompt, p_prompt[layer], wts)
        y_sample = _layer(y_sample, p_sample[layer], wts)
    return (y_prompt, y_sample)
```

```python
import functools
import math

import jax
import jax.numpy as jnp
from jax import lax
from jax.experimental import pallas as pl
from jax.experimental.pallas import tpu as pltpu

F32 = jnp.float32
BF16 = jnp.bfloat16

D_MODEL = 2048
PLE_DIM = 256
POOL_WIDTH = 1024
POOL_WINDOWS = (2, 4, 8, 16)
POOL_GROUP = 256
RWKV_WIDTH = 1024
HEAD_SIZE = 64
DECAY_LORA = 64
ICL_LORA = 64
GATE_LORA = 160
FFN_DIM = 5632
RMS_EPS = 1e-6
LNX_EPS = 64e-5
LOG2E = math.log2(math.e)
DECAY_SCALE2 = math.exp(-0.5) * LOG2E

LANES = 128
SUBLANES = 8
MXU_TILE = 256
N_PAIRS = RWKV_WIDTH // LANES
LORA_PAD = 384
GATE_PAD = LORA_PAD - LANES
PAIR_COLS = 3 * LANES
LORA_COL = POOL_WIDTH + 3 * RWKV_WIDTH
Z_COLS = LORA_COL + LORA_PAD
CHUNK = 64
HALO = SUBLANES
VMEM_LIMIT = 56 * 1024 * 1024

FFN_TM, FFN_TF = 512, 512
MIXIN_T = 256
SCAN_T = 256
POST_TM = 256
PLE_TM = 512


def _dot(a, b):
    return jnp.dot(a, b, preferred_element_type=F32)


def _dot_nt(a, b):
    return lax.dot_general(a, b, (((1,), (1,)), ((), ())), preferred_element_type=F32)


def _dot_tn(a, b):
    return lax.dot_general(a, b, (((0,), (0,)), ((), ())), preferred_element_type=F32)


def _split2(x):
    h = x.astype(BF16)
    l = (x - h.astype(F32)).astype(BF16)
    return h, l


def _dot2_lhs(x, w2):
    return _dot(jnp.concatenate(_split2(x), axis=1), w2)


def _sigmoid(x):
    return 1.0 / (1.0 + jnp.exp2(x * (-LOG2E)))


def _rms(x, g):
    ms = jnp.mean(x * x, axis=-1, keepdims=True)
    return x * lax.rsqrt(ms + RMS_EPS) * g


def _head_ones2():
    r = (lax.broadcasted_iota(jnp.int32, (2 * LANES, LANES), 0) % LANES) // HEAD_SIZE
    c = lax.broadcasted_iota(jnp.int32, (2 * LANES, LANES), 1) // HEAD_SIZE
    return jnp.where(r == c, 1.0, 0.0).astype(BF16)


def _params(sem):
    return pltpu.CompilerParams(dimension_semantics=sem, vmem_limit_bytes=VMEM_LIMIT)


def _ffn_kernel(x_ref, pre_g_ref, post_g_ref, wg_ref, wu_ref, wd_ref, o_ref, u_ref, acc_ref, *, n_f):
    j = pl.program_id(1)

    @pl.when(j == 0)
    def _():
        u_ref[...] = _rms(x_ref[...], pre_g_ref[...]).astype(BF16)
        acc_ref[...] = jnp.zeros_like(acc_ref)

    u = u_ref[...]
    gate = _dot(u, wg_ref[...])
    up = _dot(u, wu_ref[...])
    h = (gate * _sigmoid(gate)) * up
    acc_ref[...] += _dot(h.astype(BF16), wd_ref[...])

    @pl.when(j == n_f - 1)
    def _():
        o_ref[...] = x_ref[...] + 0.5 * _rms(acc_ref[...], post_g_ref[...])


def _ffn(x, pre_g, post_g, wg, wu, wd):
    n = x.shape[0]
    tm, tf = min(FFN_TM, n), FFN_TF
    grid = (n // tm, FFN_DIM // tf)
    row = pl.BlockSpec((tm, D_MODEL), lambda i, j: (i, 0))
    vec = pl.BlockSpec((1, D_MODEL), lambda i, j: (0, 0))
    return pl.pallas_call(
        functools.partial(_ffn_kernel, n_f=grid[1]),
        grid=grid,
        in_specs=[row, vec, vec,
                  pl.BlockSpec((D_MODEL, tf), lambda i, j: (0, j)),
                  pl.BlockSpec((D_MODEL, tf), lambda i, j: (0, j)),
                  pl.BlockSpec((tf, D_MODEL), lambda i, j: (j, 0))],
        out_specs=row,
        out_shape=jax.ShapeDtypeStruct((n, D_MODEL), F32),
        scratch_shapes=[pltpu.VMEM((tm, D_MODEL), BF16), pltpu.VMEM((tm, D_MODEL), F32)],
        compiler_params=_params(("parallel", "arbitrary")),
        name="ffn",
    )(x, pre_g, post_g, wg, wu, wd)


def _mixin_kernel(xc_ref, xp_ref, xn_ref, preg_ref, win_ref, mu_ref, poolw_ref, pscale_ref,
                  wup_f_ref, aup_f_ref, wup_b_ref, aup_b_ref,
                  w0f_ref, a0f_ref, w0b_ref, a0b_ref, kk_ref, ka_ref, rk_ref,
                  pool_o, v_o, bonus_o, sg_o,
                  at_f, rt_f, bt_f, kt_f, bp_f, kp_f, pe_f,
                  at_b, rt_b, bt_b, kt_b, bp_b, kp_b, pe_b,
                  u_ref, ext_ref, *, seq_len, n_tiles):
    t_tile = xc_ref.shape[0]
    n_chunks = t_tile // CHUNK
    i = pl.program_id(1)

    x_ext = jnp.concatenate([jnp.where(i > 0, xp_ref[...], 0.0), xc_ref[...],
                             jnp.where(i < n_tiles - 1, xn_ref[...], 0.0)], axis=0)
    u_ref[...] = _rms(x_ext, preg_ref[...]).astype(BF16)

    units = [(gi * POOL_GROUP, POOL_GROUP) for gi in range(len(POOL_WINDOWS))] + [(LORA_COL, LORA_PAD)]
    n_head = len(units)
    units += [(c, MXU_TILE) for c in range(POOL_WIDTH, LORA_COL, MXU_TILE)]
    issued = [0]

    def ensure(n):
        while issued[0] < min(n, len(units)):
            col, width = units[issued[0]]
            ext_ref[:, col:col + width] = _dot(u_ref[...], win_ref[:, col:col + width])
            issued[0] += 1

    def pair_units(p):
        return n_head + -(-((p + 1) * PAIR_COLS) // MXU_TILE)

    def rows(d, col, width):
        return ext_ref[HALO + d:HALO + d + t_tile, col:col + width]

    ensure(2)

    pos = i * t_tile + lax.broadcasted_iota(jnp.int32, (t_tile, 1), 0)
    for gi, win in enumerate(POOL_WINDOWS):
        half = win // 2
        col = gi * POOL_GROUP
        tot = rows(-half, col, POOL_GROUP)
        for d in range(-half + 1, win - half):
            tot = tot + rows(d, col, POOL_GROUP)
        cnt = (jnp.minimum(pos + (win - half), seq_len) - jnp.maximum(pos - half, 0)).astype(F32)
        pooled = tot / cnt - rows(0, col, POOL_GROUP)
        ensure(gi + 3)
        mixed = _dot(pooled.astype(BF16), poolw_ref[gi])
        pool_o[:, col:col + POOL_GROUP] = (mixed * pscale_ref[:, col:col + POOL_GROUP]).astype(BF16)

    def shifted(col, width):
        mu = mu_ref[:, col - POOL_WIDTH:col - POOL_WIDTH + width]
        return (1.0 - mu) * rows(0, col, width) + (0.5 * mu) * (rows(-1, col, width) + rows(1, col, width))

    ensure(pair_units(0))
    lora = shifted(LORA_COL, LORA_PAD)
    dwa = lora[:, 0:LANES]
    tw = jnp.tanh(dwa).astype(BF16)
    da = dwa.astype(BF16)
    sg_o[...] = _sigmoid(lora[:, LANES:]).astype(BF16)

    rr = lax.broadcasted_iota(jnp.int32, (t_tile, t_tile), 0)
    cc = lax.broadcasted_iota(jnp.int32, (t_tile, t_tile), 1)
    same = (rr // CHUNK) == (cc // CHUNK)
    m_fwd = jnp.where(same & (cc <= rr), 1.0, 0.0).astype(BF16)
    m_bwd = jnp.where(same & (cc >= rr), 1.0, 0.0).astype(BF16)
    ones_bd2 = _head_ones2()

    dirs = ((w0f_ref, a0f_ref, m_fwd, CHUNK - 1, (at_f, rt_f, bt_f, kt_f, bp_f, kp_f, pe_f)),
            (w0b_ref, a0b_ref, m_bwd, 0, (at_b, rt_b, bt_b, kt_b, bp_b, kp_b, pe_b)))

    for p in range(N_PAIRS):
        ls = slice(p * LANES, (p + 1) * LANES)
        ensure(pair_units(p))

        def trickle(p=p):
            ensure(min(issued[0] + 1, pair_units(p + 1)))

        col = POOL_WIDTH + p * PAIR_COLS
        r = shifted(col, LANES)
        k = shifted(col + LANES, LANES)
        v = shifted(col + 2 * LANES, LANES)
        v_o[:, ls] = v.astype(BF16)

        kkr = k * kk_ref[:, ls]
        trickle()
        nrm = jnp.sqrt(_dot2_lhs(kkr * kkr, ones_bd2))
        kk = kkr / jnp.maximum(nrm, 1e-12)
        nkk = -kk

        xw_fb = _dot(tw, jnp.concatenate([wup_f_ref[:, ls], wup_b_ref[:, ls]], axis=1))
        xa_fb = _dot(da, jnp.concatenate([aup_f_ref[:, ls], aup_b_ref[:, ls]], axis=1))

        ksum = None
        for di, (w0_ref, a0_ref, m_dir, end_row, outs) in enumerate(dirs):
            o_at, o_rt, o_bt, o_kt, o_bp, o_kp, o_pe = outs
            dl = slice(di * LANES, (di + 1) * LANES)
            xw = w0_ref[:, ls] + xw_fb[:, dl]
            lw2 = -DECAY_SCALE2 * _sigmoid(xw)
            a = _sigmoid(a0_ref[:, ls] + xa_fb[:, dl])
            kd = k * (1.0 + (a - 1.0) * ka_ref[:, ls])
            ksum = kd if ksum is None else ksum + kd
            beta = kk * a
            trickle()
            lp2 = _dot(m_dir, jnp.concatenate(_split2(lw2), axis=1))
            lp = lp2[:, 0:LANES] + lp2[:, LANES:]
            ends = [lp[c * CHUNK + end_row:c * CHUNK + end_row + 1, :] for c in range(n_chunks)]
            lpe = jnp.concatenate([jnp.broadcast_to(e, (CHUNK, LANES)) for e in ends], axis=0)
            e_inv = jnp.exp2(-lp)
            e_end = jnp.exp2(lpe - lp)
            o_at[:, ls] = (jnp.exp2(lp - lw2) * nkk).astype(BF16)
            o_rt[:, ls] = (jnp.exp2(lp) * r).astype(BF16)
            o_bt[:, ls] = (e_inv * beta).astype(BF16)
            o_kt[:, ls] = (e_inv * kd).astype(BF16)
            o_bp[:, ls] = (e_end * beta).astype(BF16)
            o_kp[:, ls] = (e_end * kd).astype(BF16)
            for c in range(n_chunks):
                o_pe[c, :, ls] = jnp.exp2(ends[c])

        bonus_o[:, ls] = _dot2_lhs(r * ksum * rk_ref[:, ls], ones_bd2) * v


def _mixin(x, wts, seq_len):
    b = x.shape[0]
    t = min(MIXIN_T, seq_len)
    n_tiles = seq_len // t
    n_halo = t // HALO
    last_halo = seq_len // HALO - 1
    cur = pl.BlockSpec((None, t, D_MODEL), lambda bi, i: (bi, i, 0))
    prev = pl.BlockSpec((None, HALO, D_MODEL), lambda bi, i: (bi, jnp.maximum(i * n_halo - 1, 0), 0))
    nxt = pl.BlockSpec((None, HALO, D_MODEL), lambda bi, i: (bi, jnp.minimum((i + 1) * n_halo, last_halo), 0))

    def const(shape):
        nd = len(shape)
        return pl.BlockSpec(shape, lambda bi, i: (0,) * nd, pipeline_mode=pl.Buffered(1))

    tok = pl.BlockSpec((None, t, RWKV_WIDTH), lambda bi, i: (bi, i, 0))
    pe = pl.BlockSpec((None, t // CHUNK, 1, RWKV_WIDTH), lambda bi, i: (bi, i, 0, 0))
    tok_bf = jax.ShapeDtypeStruct((b, seq_len, RWKV_WIDTH), BF16)
    pe_shape = jax.ShapeDtypeStruct((b, seq_len // CHUNK, 1, RWKV_WIDTH), F32)
    vec = const((1, RWKV_WIDTH))
    lora_w = const((LANES, RWKV_WIDTH))
    in_specs = [cur, prev, nxt, const((1, D_MODEL)), const((D_MODEL, Z_COLS)),
                const((1, Z_COLS - POOL_WIDTH)), const((4, POOL_GROUP, POOL_GROUP)), const((1, POOL_WIDTH)),
                lora_w, lora_w, lora_w, lora_w,
                vec, vec, vec, vec, vec, vec, vec]
    out_specs = [pl.BlockSpec((None, t, POOL_WIDTH), lambda bi, i: (bi, i, 0)), tok, tok,
                 pl.BlockSpec((None, t, GATE_PAD), lambda bi, i: (bi, i, 0))] + 2 * ([tok] * 6 + [pe])
    out_shape = [jax.ShapeDtypeStruct((b, seq_len, POOL_WIDTH), BF16), tok_bf,
                 jax.ShapeDtypeStruct((b, seq_len, RWKV_WIDTH), F32),
                 jax.ShapeDtypeStruct((b, seq_len, GATE_PAD), BF16)] + 2 * ([tok_bf] * 6 + [pe_shape])
    return pl.pallas_call(
        functools.partial(_mixin_kernel, seq_len=seq_len, n_tiles=n_tiles),
        grid=(b, n_tiles),
        in_specs=in_specs,
        out_specs=out_specs,
        out_shape=out_shape,
        scratch_shapes=[pltpu.VMEM((t + 2 * HALO, D_MODEL), BF16), pltpu.VMEM((t + 2 * HALO, Z_COLS), F32)],
        compiler_params=_params(("parallel", "arbitrary")),
        name="mixin",
    )(x, x, x, *wts)


def _scan_kernel(v_f, at_f, rt_f, bt_f, kt_f, bp_f, kp_f, pe_f,
                 v_b, at_b, rt_b, bt_b, kt_b, bp_b, kp_b, pe_b,
                 y_f, y_b, s_ref):
    n_chunks = v_f.shape[0] // CHUNK

    @pl.when(pl.program_id(1) == 0)
    def _():
        s_ref[...] = jnp.zeros_like(s_ref)

    lane = lax.broadcasted_iota(jnp.int32, (CHUNK, LANES), 1)
    head0 = lane < HEAD_SIZE
    tt = lax.broadcasted_iota(jnp.int32, (CHUNK, LANES), 0)
    ss = lane % HEAD_SIZE
    r2 = lax.broadcasted_iota(jnp.int32, (LANES, LANES), 0)
    c2 = lax.broadcasted_iota(jnp.int32, (LANES, LANES), 1)
    bd = (r2 // HEAD_SIZE) == (c2 // HEAD_SIZE)
    eye_h = jnp.where(tt == ss, 1.0, 0.0).astype(F32)
    n_double = int(math.log2(CHUNK)) - 1

    def stack2(x):
        zero = jnp.zeros_like(x)
        return jnp.concatenate([jnp.where(head0, x, zero), jnp.where(head0, zero, x)], axis=0)

    fwd_refs = (v_f, at_f, rt_f, bt_f, kt_f, bp_f, kp_f, pe_f)
    bwd_refs = (v_b, at_b, rt_b, bt_b, kt_b, bp_b, kp_b, pe_b)

    def blk(m_h):
        return jnp.where(bd, jnp.concatenate([m_h, m_h], axis=0), jnp.zeros((), m_h.dtype))

    steps = []
    for k in range(n_chunks):
        chains = []
        for d, refs, y_ref, c, strict, incl in ((0, fwd_refs, y_f, k, ss < tt, ss <= tt),
                                                (1, bwd_refs, y_b, n_chunks - 1 - k, ss > tt, ss >= tt)):
            rs = slice(c * CHUNK, (c + 1) * CHUNK)
            for p in range(N_PAIRS):
                ls = slice(p * LANES, (p + 1) * LANES)
                chains.append(dict(d=d, p=p, rs=rs, ls=ls, c=c, refs=refs, y_ref=y_ref, strict=strict, incl=incl))
        steps.append(chains)
    every = [ch for chains in steps for ch in chains]

    for ch in every:
        v_r, at_r, rt_r, bt_r, kt_r, _, _, _ = ch["refs"]
        rs, ls = ch["rs"], ch["ls"]
        hh = _dot_nt(jnp.concatenate([at_r[rs, ls], rt_r[rs, ls]], axis=0),
                     jnp.concatenate([stack2(bt_r[rs, ls]), stack2(kt_r[rs, ls])], axis=0))
        h_ab = jnp.where(ch["strict"], hh[0:CHUNK, 0:LANES], 0.0)
        ch["h_kv"] = jnp.concatenate([jnp.where(ch["strict"], hh[0:CHUNK, LANES:], 0.0),
                                      jnp.where(ch["incl"], hh[CHUNK:, LANES:], 0.0)], axis=0).astype(BF16)
        ch["h_rb"] = jnp.where(ch["incl"], hh[CHUNK:, 0:LANES], 0.0).astype(BF16)
        ch["x"] = h_ab.astype(BF16)
        ch["t"] = eye_h + h_ab

    for ch in every:
        ch["m2"] = _dot(ch["h_kv"], stack2(ch["refs"][0][ch["rs"], ch["ls"]]))
        ch["x"] = _dot(ch["x"], blk(ch["x"])).astype(BF16)

    for k in range(n_double):
        for ch in every:
            x_blk = blk(ch["x"])
            if k + 1 < n_double:
                tx = _dot(jnp.concatenate([ch["t"].astype(BF16), ch["x"]], axis=0), x_blk)
                ch["t"] = ch["t"] + tx[0:CHUNK]
                ch["x"] = tx[CHUNK:].astype(BF16)
            else:
                ch["t"] = (ch["t"] + _dot(ch["t"].astype(BF16), x_blk)).astype(BF16)

    state = {(ch["d"], ch["p"]): s_ref[ch["d"], ch["p"]] for ch in steps[0]}
    for chains in steps:
        for ch in chains:
            _, at_r, rt_r = ch["refs"][:3]
            rs, ls = ch["rs"], ch["ls"]
            st_b = state[ch["d"], ch["p"]].astype(BF16)
            m1 = _dot(jnp.concatenate([at_r[rs, ls], rt_r[rs, ls]], axis=0), st_b)
            ch["w"] = (m1[0:CHUNK] + ch["m2"][0:CHUNK]).astype(BF16)
            ch["y0"] = m1[CHUNK:] + ch["m2"][CHUNK:]
        for ch in chains:
            ch["ub"] = _dot(ch["t"], stack2(ch["w"])).astype(BF16)
        for ch in chains:
            v_r, _, _, _, _, bp_r, kp_r, pe_r = ch["refs"]
            rs, ls = ch["rs"], ch["ls"]
            ch["y_ref"][rs, ls] = ch["y0"] + _dot(ch["h_rb"], stack2(ch["ub"]))
            upd = _dot_tn(jnp.concatenate([bp_r[rs, ls], kp_r[rs, ls]], axis=0),
                          jnp.concatenate([ch["ub"], v_r[rs, ls]], axis=0))
            pend_col = jnp.broadcast_to(pe_r[ch["c"]][:, ls], (LANES, LANES)).T
            state[ch["d"], ch["p"]] = state[ch["d"], ch["p"]] * pend_col + jnp.where(bd, upd, 0.0)

    for (d, p), st in state.items():
        s_ref[d, p] = st


def _scan(v, fwd, bwd, seq_len):
    b = v.shape[0]
    t = min(SCAN_T, seq_len)
    n_tiles = seq_len // t
    nc = t // CHUNK

    def tok(rev):
        if rev:
            return pl.BlockSpec((None, t, RWKV_WIDTH), lambda bi, i: (bi, n_tiles - 1 - i, 0))
        return pl.BlockSpec((None, t, RWKV_WIDTH), lambda bi, i: (bi, i, 0))

    def pe(rev):
        if rev:
            return pl.BlockSpec((None, nc, 1, RWKV_WIDTH), lambda bi, i: (bi, n_tiles - 1 - i, 0, 0))
        return pl.BlockSpec((None, nc, 1, RWKV_WIDTH), lambda bi, i: (bi, i, 0, 0))

    in_specs = [tok(False)] * 7 + [pe(False)] + [tok(True)] * 7 + [pe(True)]
    y_shape = jax.ShapeDtypeStruct((b, seq_len, RWKV_WIDTH), F32)
    return pl.pallas_call(
        _scan_kernel,
        grid=(b, n_tiles),
        in_specs=in_specs,
        out_specs=[tok(False), tok(True)],
        out_shape=[y_shape, y_shape],
        scratch_shapes=[pltpu.VMEM((2, N_PAIRS, LANES, LANES), F32)],
        compiler_params=_params(("parallel", "arbitrary")),
        name="scan",
    )(v, *fwd, v, *bwd)


def _post_kernel(x_ref, yf_ref, yb_ref, bonus_ref, sg_ref, pool_ref, lnw_ref, lnb_ref, gup_ref,
                 wout_ref, postg_ref, o_ref, mix_ref):
    ones_bd = _head_ones2()
    mix_ref[:, 0:POOL_WIDTH] = pool_ref[...]
    inv_n = 1.0 / HEAD_SIZE
    gate = _dot(sg_ref[...], gup_ref[...])
    for p in range(N_PAIRS):
        ls = slice(p * LANES, (p + 1) * LANES)
        y = yf_ref[:, ls] + yb_ref[:, ls]
        mean = _dot2_lhs(y, ones_bd) * inv_n
        yc = y - mean
        var = _dot2_lhs(yc * yc, ones_bd) * inv_n
        yn = yc * lax.rsqrt(var + LNX_EPS) * lnw_ref[:, ls] + lnb_ref[:, ls]
        out = (yn + bonus_ref[:, ls]) * gate[:, ls]
        mix_ref[:, POOL_WIDTH + p * LANES:POOL_WIDTH + (p + 1) * LANES] = out.astype(BF16)
    mixed = _dot(mix_ref[...], wout_ref[...])
    o_ref[...] = x_ref[...] + _rms(mixed, postg_ref[...])


def _post(x, yf, yb, bonus, sg, pool, lnw, lnb, gup, wout, postg):
    n = x.shape[0]
    tm = min(POST_TM, n)
    row = pl.BlockSpec((tm, D_MODEL), lambda i: (i, 0))
    half = pl.BlockSpec((tm, RWKV_WIDTH), lambda i: (i, 0))
    vec_h = pl.BlockSpec((1, RWKV_WIDTH), lambda i: (0, 0))
    return pl.pallas_call(
        _post_kernel,
        grid=(n // tm,),
        in_specs=[row, half, half, half, pl.BlockSpec((tm, GATE_PAD), lambda i: (i, 0)), half, vec_h, vec_h,
                  pl.BlockSpec((GATE_PAD, RWKV_WIDTH), lambda i: (0, 0)),
                  pl.BlockSpec((D_MODEL, D_MODEL), lambda i: (0, 0)),
                  pl.BlockSpec((1, D_MODEL), lambda i: (0, 0))],
        out_specs=row,
        out_shape=jax.ShapeDtypeStruct((n, D_MODEL), F32),
        scratch_shapes=[pltpu.VMEM((tm, D_MODEL), BF16)],
        compiler_params=_params(("parallel",)),
        name="post",
    )(x, yf, yb, bonus, sg, pool, lnw, lnb, gup, wout, postg)


def _ple_kernel(x_ref, p_ref, preg_ref, postg_ref, wgate_ref, wproj_ref, o_ref):
    x = x_ref[...]
    gate = _sigmoid(_dot(_rms(x, preg_ref[...]).astype(BF16), wgate_ref[...]))
    proj = _dot(p_ref[...].astype(BF16), wproj_ref[...])
    o_ref[...] = x + _rms(gate * proj, postg_ref[...])


def _ple(x, p, preg, postg, wgate, wproj):
    n = x.shape[0]
    tm = min(PLE_TM, n)
    row = pl.BlockSpec((tm, D_MODEL), lambda i: (i, 0))
    vec = pl.BlockSpec((1, D_MODEL), lambda i: (0, 0))
    return pl.pallas_call(
        _ple_kernel,
        grid=(n // tm,),
        in_specs=[row, pl.BlockSpec((tm, PLE_DIM), lambda i: (i, 0)), vec, vec,
                  pl.BlockSpec((D_MODEL, D_MODEL), lambda i: (0, 0)),
                  pl.BlockSpec((PLE_DIM, D_MODEL), lambda i: (0, 0))],
        out_specs=row,
        out_shape=jax.ShapeDtypeStruct((n, D_MODEL), F32),
        compiler_params=_params(("parallel",)),
        name="ple",
    )(x, p, preg, postg, wgate, wproj)


def _pad_rows(w, before, total):
    return jnp.pad(w, ((before, total - before - w.shape[0]), (0, 0)))


def _prepare_weights(ffn1_pre_g, ffn1_post_g, ffn1_w_gate, ffn1_w_up, ffn1_w_down,
                     mix_pre_g, mix_post_g, w_in, mu_shift, pool_w, pool_scale,
                     w0_f, w_up_f, a0_f, a_up_f, w0_b, w_up_b, a0_b, a_up_b,
                     g_up, k_k, k_a, r_k, lnx_w, lnx_b, w_out,
                     ffn2_pre_g, ffn2_post_g, ffn2_w_gate, ffn2_w_up, ffn2_w_down,
                     ple_pre_g, ple_post_g, ple_gate_w, ple_proj_w):
    row = lambda a: a.reshape(1, -1)
    bf = lambda a: a.astype(BF16)
    pad_c = Z_COLS - w_in.shape[1]

    def by_pair(a):
        lead = a.shape[:-1]
        rkv = a[..., POOL_WIDTH:LORA_COL].reshape(lead + (3, N_PAIRS, LANES))
        rkv = jnp.swapaxes(rkv, -3, -2).reshape(lead + (3 * RWKV_WIDTH,))
        out = jnp.concatenate([a[..., :POOL_WIDTH], rkv, a[..., LORA_COL:]], axis=-1)
        return jnp.pad(out, [(0, 0)] * len(lead) + [(0, pad_c)])

    mu_full = jnp.concatenate([jnp.zeros((POOL_WIDTH,), F32), mu_shift])
    mixin = (row(mix_pre_g), bf(by_pair(w_in)), row(by_pair(mu_full)[POOL_WIDTH:]), bf(pool_w), row(pool_scale),
             bf(_pad_rows(w_up_f, 0, LANES)), bf(_pad_rows(a_up_f, DECAY_LORA, LANES)),
             bf(_pad_rows(w_up_b, 0, LANES)), bf(_pad_rows(a_up_b, DECAY_LORA, LANES)),
             row(w0_f), row(a0_f), row(w0_b), row(a0_b), row(k_k), row(k_a), row(r_k))
    return dict(
        ffn1=(row(ffn1_pre_g), row(ffn1_post_g), bf(ffn1_w_gate), bf(ffn1_w_up), bf(ffn1_w_down)),
        mixin=mixin,
        post=(row(lnx_w), row(lnx_b), bf(_pad_rows(g_up, 0, GATE_PAD)), bf(w_out), row(mix_post_g)),
        ffn2=(row(ffn2_pre_g), row(ffn2_post_g), bf(ffn2_w_gate), bf(ffn2_w_up), bf(ffn2_w_down)),
        ple=(row(ple_pre_g), row(ple_post_g), bf(ple_gate_w), bf(ple_proj_w)),
    )


def _layer(x, p, wts):
    b, seq_len, _ = x.shape
    n = b * seq_len
    x = _ffn(x.reshape(n, D_MODEL), *wts["ffn1"])
    outs = _mixin(x.reshape(b, seq_len, D_MODEL), wts["mixin"], seq_len)
    pool, v, bonus, sg = outs[:4]
    fwd, bwd = outs[4:11], outs[11:18]
    yf, yb = _scan(v, fwd, bwd, seq_len)
    flat = lambda a: a.reshape(n, a.shape[-1])
    x = _post(x, flat(yf), flat(yb), flat(bonus), flat(sg), flat(pool), *wts["post"])
    x = _ffn(x, *wts["ffn2"])
    x = _ple(x, p.reshape(n, PLE_DIM), *wts["ple"])
    return x.reshape(b, seq_len, D_MODEL)


def kernel(x_prompt, x_sample, p_prompt, p_sample, ffn1_pre_g, ffn1_post_g, ffn1_w_gate, ffn1_w_up, ffn1_w_down, mix_pre_g, mix_post_g, w_in, mu_shift, pool_w, pool_scale, w0_f, w_up_f, a0_f, a_up_f, w0_b, w_up_b, a0_b, a_up_b, g_up, k_k, k_a, r_k, lnx_w, lnx_b, w_out, ffn2_pre_g, ffn2_post_g, ffn2_w_gate, ffn2_w_up, ffn2_w_down, ple_pre_g, ple_post_g, ple_gate_w, ple_proj_w):
    weights = (ffn1_pre_g, ffn1_post_g, ffn1_w_gate, ffn1_w_up, ffn1_w_down, mix_pre_g, mix_post_g, w_in,
               mu_shift, pool_w, pool_scale, w0_f, w_up_f, a0_f, a_up_f, w0_b, w_up_b, a0_b, a_up_b,
               g_up, k_k, k_a, r_k, lnx_w, lnx_b, w_out, ffn2_pre_g, ffn2_post_g, ffn2_w_gate, ffn2_w_up,
               ffn2_w_down, ple_pre_g, ple_post_g, ple_gate_w, ple_proj_w)
    y_prompt, y_sample = x_prompt, x_sample
    for layer in range(ffn1_pre_g.shape[0]):
        wts = _prepare_weights(*(w[layer] for w in weights))
        y_prompt = _layer(y_prompt, p_prompt[layer], wts)
        y_sample = _layer(y_sample, p_sample[layer], wts)
    return (y_prompt, y_sample)
```

```python
import functools
import math

import jax
import jax.numpy as jnp
from jax import lax
from jax.experimental import pallas as pl
from jax.experimental.pallas import tpu as pltpu

F32 = jnp.float32
BF16 = jnp.bfloat16

D_MODEL = 2048
PLE_DIM = 256
POOL_WIDTH = 1024
POOL_WINDOWS = (2, 4, 8, 16)
POOL_GROUP = 256
RWKV_WIDTH = 1024
HEAD_SIZE = 64
DECAY_LORA = 64
ICL_LORA = 64
GATE_LORA = 160
FFN_DIM = 5632
RMS_EPS = 1e-6
LNX_EPS = 64e-5
LOG2E = math.log2(math.e)
DECAY_SCALE2 = math.exp(-0.5) * LOG2E

LANES = 128
SUBLANES = 8
MXU_TILE = 256
N_PAIRS = RWKV_WIDTH // LANES
LORA_PAD = 384
GATE_PAD = LORA_PAD - LANES
PAIR_COLS = 3 * LANES
LORA_COL = POOL_WIDTH + 3 * RWKV_WIDTH
Z_COLS = LORA_COL + LORA_PAD
CHUNK = 64
HALO = SUBLANES
VMEM_LIMIT = 56 * 1024 * 1024

FFN_TM, FFN_TF = 1024, 512
MIXIN_T = 256
SCAN_T = 256
POST_TM = 512
PLE_TM = 512


def _dot(a, b):
    return jnp.dot(a, b, preferred_element_type=F32)


def _dot_nt(a, b):
    return lax.dot_general(a, b, (((1,), (1,)), ((), ())), preferred_element_type=F32)


def _dot_tn(a, b):
    return lax.dot_general(a, b, (((0,), (0,)), ((), ())), preferred_element_type=F32)


def _split2(x):
    h = x.astype(BF16)
    l = (x - h.astype(F32)).astype(BF16)
    return h, l


def _dot2_lhs(x, w2):
    return _dot(jnp.concatenate(_split2(x), axis=1), w2)


def _sigmoid(x):
    return 1.0 / (1.0 + jnp.exp2(x * (-LOG2E)))


def _rms(x, g):
    ms = jnp.mean(x * x, axis=-1, keepdims=True)
    return x * lax.rsqrt(ms + RMS_EPS) * g


def _head_ones2():
    r = (lax.broadcasted_iota(jnp.int32, (2 * LANES, LANES), 0) % LANES) // HEAD_SIZE
    c = lax.broadcasted_iota(jnp.int32, (2 * LANES, LANES), 1) // HEAD_SIZE
    return jnp.where(r == c, 1.0, 0.0).astype(BF16)


def _params(sem):
    return pltpu.CompilerParams(dimension_semantics=sem, vmem_limit_bytes=VMEM_LIMIT)


def _ffn_kernel(x_ref, pre_g_ref, post_g_ref, wg_ref, wu_ref, wd_ref, o_ref, u_ref, *, n_f):
    j = pl.program_id(1)

    def down_proj():
        u = u_ref[...]
        gate = _dot(u, wg_ref[...])
        up = _dot(u, wu_ref[...])
        h = (gate * _sigmoid(gate)) * up
        return _dot(h.astype(BF16), wd_ref[...])

    @pl.when(j == 0)
    def _():
        u_ref[...] = _rms(x_ref[...], pre_g_ref[...]).astype(BF16)
        o_ref[...] = down_proj()

    @pl.when(j > 0)
    def _():
        o_ref[...] += down_proj()

    @pl.when(j == n_f - 1)
    def _():
        o_ref[...] = x_ref[...] + 0.5 * _rms(o_ref[...], post_g_ref[...])


def _ffn(x, pre_g, post_g, wg, wu, wd):
    n = x.shape[0]
    tm, tf = min(FFN_TM, n), FFN_TF
    grid = (n // tm, FFN_DIM // tf)
    row = pl.BlockSpec((tm, D_MODEL), lambda i, j: (i, 0))
    vec = pl.BlockSpec((1, D_MODEL), lambda i, j: (0, 0))
    return pl.pallas_call(
        functools.partial(_ffn_kernel, n_f=grid[1]),
        grid=grid,
        in_specs=[row, vec, vec,
                  pl.BlockSpec((D_MODEL, tf), lambda i, j: (0, j)),
                  pl.BlockSpec((D_MODEL, tf), lambda i, j: (0, j)),
                  pl.BlockSpec((tf, D_MODEL), lambda i, j: (j, 0))],
        out_specs=row,
        out_shape=jax.ShapeDtypeStruct((n, D_MODEL), F32),
        scratch_shapes=[pltpu.VMEM((tm, D_MODEL), BF16)],
        compiler_params=_params(("parallel", "arbitrary")),
        name="ffn",
    )(x, pre_g, post_g, wg, wu, wd)


def _mixin_kernel(xc_ref, xp_ref, xn_ref, preg_ref, win_ref, mu_ref, poolw_ref, pscale_ref,
                  wup_f_ref, aup_f_ref, wup_b_ref, aup_b_ref,
                  w0f_ref, a0f_ref, w0b_ref, a0b_ref, kk_ref, ka_ref, rk_ref,
                  pool_o, v_o, bonus_o, sg_o,
                  at_f, rt_f, bt_f, kt_f, bp_f, kp_f, pe_f,
                  at_b, rt_b, bt_b, kt_b, bp_b, kp_b, pe_b,
                  u_ref, ext_ref, *, seq_len, n_tiles):
    t_tile = xc_ref.shape[0]
    n_chunks = t_tile // CHUNK
    i = pl.program_id(1)

    x_ext = jnp.concatenate([jnp.where(i > 0, xp_ref[...], 0.0), xc_ref[...],
                             jnp.where(i < n_tiles - 1, xn_ref[...], 0.0)], axis=0)
    u_ref[...] = _rms(x_ext, preg_ref[...]).astype(BF16)

    units = [(gi * POOL_GROUP, POOL_GROUP) for gi in range(len(POOL_WINDOWS))] + [(LORA_COL, LORA_PAD)]
    n_head = len(units)
    units += [(c, MXU_TILE) for c in range(POOL_WIDTH, LORA_COL, MXU_TILE)]
    issued = [0]

    def ensure(n):
        while issued[0] < min(n, len(units)):
            col, width = units[issued[0]]
            ext_ref[:, col:col + width] = _dot(u_ref[...], win_ref[:, col:col + width])
            issued[0] += 1

    def pair_units(p):
        return n_head + -(-((p + 1) * PAIR_COLS) // MXU_TILE)

    def rows(d, col, width):
        return ext_ref[HALO + d:HALO + d + t_tile, col:col + width]

    ensure(2)

    pos = i * t_tile + lax.broadcasted_iota(jnp.int32, (t_tile, 1), 0)
    for gi, win in enumerate(POOL_WINDOWS):
        half = win // 2
        col = gi * POOL_GROUP
        tot = rows(-half, col, POOL_GROUP)
        for d in range(-half + 1, win - half):
            tot = tot + rows(d, col, POOL_GROUP)
        cnt = (jnp.minimum(pos + (win - half), seq_len) - jnp.maximum(pos - half, 0)).astype(F32)
        pooled = tot / cnt - rows(0, col, POOL_GROUP)
        ensure(gi + 3)
        mixed = _dot(pooled.astype(BF16), poolw_ref[gi])
        pool_o[:, col:col + POOL_GROUP] = (mixed * pscale_ref[:, col:col + POOL_GROUP]).astype(BF16)

    def shifted(col, width):
        mu = mu_ref[:, col - POOL_WIDTH:col - POOL_WIDTH + width]
        return (1.0 - mu) * rows(0, col, width) + (0.5 * mu) * (rows(-1, col, width) + rows(1, col, width))

    ensure(pair_units(0))
    lora = shifted(LORA_COL, LORA_PAD)
    dwa = lora[:, 0:LANES]
    tw = jnp.tanh(dwa).astype(BF16)
    da = dwa.astype(BF16)
    sg_o[...] = _sigmoid(lora[:, LANES:]).astype(BF16)

    rr = lax.broadcasted_iota(jnp.int32, (t_tile, t_tile), 0)
    cc = lax.broadcasted_iota(jnp.int32, (t_tile, t_tile), 1)
    same = (rr // CHUNK) == (cc // CHUNK)
    m_fwd = jnp.where(same & (cc <= rr), 1.0, 0.0).astype(BF16)
    m_bwd = jnp.where(same & (cc >= rr), 1.0, 0.0).astype(BF16)
    ones_bd2 = _head_ones2()

    dirs = ((w0f_ref, a0f_ref, m_fwd, CHUNK - 1, (at_f, rt_f, bt_f, kt_f, bp_f, kp_f, pe_f)),
            (w0b_ref, a0b_ref, m_bwd, 0, (at_b, rt_b, bt_b, kt_b, bp_b, kp_b, pe_b)))

    for p in range(N_PAIRS):
        ls = slice(p * LANES, (p + 1) * LANES)
        ensure(pair_units(p))

        def trickle(p=p):
            ensure(min(issued[0] + 1, pair_units(p + 1)))

        col = POOL_WIDTH + p * PAIR_COLS
        r = shifted(col, LANES)
        k = shifted(col + LANES, LANES)
        v = shifted(col + 2 * LANES, LANES)
        v_o[:, ls] = v.astype(BF16)

        kkr = k * kk_ref[:, ls]
        trickle()
        nrm = jnp.sqrt(_dot2_lhs(kkr * kkr, ones_bd2))
        kk = kkr / jnp.maximum(nrm, 1e-12)
        nkk = -kk

        xw_fb = _dot(tw, jnp.concatenate([wup_f_ref[:, ls], wup_b_ref[:, ls]], axis=1))
        xa_fb = _dot(da, jnp.concatenate([aup_f_ref[:, ls], aup_b_ref[:, ls]], axis=1))

        ksum = None
        for di, (w0_ref, a0_ref, m_dir, end_row, outs) in enumerate(dirs):
            o_at, o_rt, o_bt, o_kt, o_bp, o_kp, o_pe = outs
            dl = slice(di * LANES, (di + 1) * LANES)
            xw = w0_ref[:, ls] + xw_fb[:, dl]
            lw2 = -DECAY_SCALE2 * _sigmoid(xw)
            a = _sigmoid(a0_ref[:, ls] + xa_fb[:, dl])
            kd = k * (1.0 + (a - 1.0) * ka_ref[:, ls])
            ksum = kd if ksum is None else ksum + kd
            beta = kk * a
            trickle()
            lp2 = _dot(m_dir, jnp.concatenate(_split2(lw2), axis=1))
            lp = lp2[:, 0:LANES] + lp2[:, LANES:]
            ends = [lp[c * CHUNK + end_row:c * CHUNK + end_row + 1, :] for c in range(n_chunks)]
            lpe = jnp.concatenate([jnp.broadcast_to(e, (CHUNK, LANES)) for e in ends], axis=0)
            e_inv = jnp.exp2(-lp)
            e_end = jnp.exp2(lpe - lp)
            o_at[:, ls] = (jnp.exp2(lp - lw2) * nkk).astype(BF16)
            o_rt[:, ls] = (jnp.exp2(lp) * r).astype(BF16)
            o_bt[:, ls] = (e_inv * beta).astype(BF16)
            o_kt[:, ls] = (e_inv * kd).astype(BF16)
            o_bp[:, ls] = (e_end * beta).astype(BF16)
            o_kp[:, ls] = (e_end * kd).astype(BF16)
            for c in range(n_chunks):
                o_pe[c, :, ls] = jnp.exp2(ends[c])

        bonus_o[:, ls] = _dot2_lhs(r * ksum * rk_ref[:, ls], ones_bd2) * v


def _mixin(x, wts, seq_len):
    b = x.shape[0]
    t = min(MIXIN_T, seq_len)
    n_tiles = seq_len // t
    n_halo = t // HALO
    last_halo = seq_len // HALO - 1
    cur = pl.BlockSpec((None, t, D_MODEL), lambda bi, i: (bi, i, 0))
    prev = pl.BlockSpec((None, HALO, D_MODEL), lambda bi, i: (bi, jnp.maximum(i * n_halo - 1, 0), 0))
    nxt = pl.BlockSpec((None, HALO, D_MODEL), lambda bi, i: (bi, jnp.minimum((i + 1) * n_halo, last_halo), 0))

    def const(shape):
        nd = len(shape)
        return pl.BlockSpec(shape, lambda bi, i: (0,) * nd, pipeline_mode=pl.Buffered(1))

    tok = pl.BlockSpec((None, t, RWKV_WIDTH), lambda bi, i: (bi, i, 0))
    pe = pl.BlockSpec((None, t // CHUNK, 1, RWKV_WIDTH), lambda bi, i: (bi, i, 0, 0))
    tok_bf = jax.ShapeDtypeStruct((b, seq_len, RWKV_WIDTH), BF16)
    pe_shape = jax.ShapeDtypeStruct((b, seq_len // CHUNK, 1, RWKV_WIDTH), F32)
    vec = const((1, RWKV_WIDTH))
    lora_w = const((LANES, RWKV_WIDTH))
    in_specs = [cur, prev, nxt, const((1, D_MODEL)), const((D_MODEL, Z_COLS)),
                const((1, Z_COLS - POOL_WIDTH)), const((4, POOL_GROUP, POOL_GROUP)), const((1, POOL_WIDTH)),
                lora_w, lora_w, lora_w, lora_w,
                vec, vec, vec, vec, vec, vec, vec]
    out_specs = [pl.BlockSpec((None, t, POOL_WIDTH), lambda bi, i: (bi, i, 0)), tok, tok,
                 pl.BlockSpec((None, t, GATE_PAD), lambda bi, i: (bi, i, 0))] + 2 * ([tok] * 6 + [pe])
    out_shape = [jax.ShapeDtypeStruct((b, seq_len, POOL_WIDTH), BF16), tok_bf,
                 jax.ShapeDtypeStruct((b, seq_len, RWKV_WIDTH), F32),
                 jax.ShapeDtypeStruct((b, seq_len, GATE_PAD), BF16)] + 2 * ([tok_bf] * 6 + [pe_shape])
    return pl.pallas_call(
        functools.partial(_mixin_kernel, seq_len=seq_len, n_tiles=n_tiles),
        grid=(b, n_tiles),
        in_specs=in_specs,
        out_specs=out_specs,
        out_shape=out_shape,
        scratch_shapes=[pltpu.VMEM((t + 2 * HALO, D_MODEL), BF16), pltpu.VMEM((t + 2 * HALO, Z_COLS), F32)],
        compiler_params=_params(("parallel", "arbitrary")),
        name="mixin",
    )(x, x, x, *wts)


def _scan_kernel(v_f, at_f, rt_f, bt_f, kt_f, bp_f, kp_f, pe_f,
                 v_b, at_b, rt_b, bt_b, kt_b, bp_b, kp_b, pe_b,
                 y_f, y_b, s_ref):
    n_chunks = v_f.shape[0] // CHUNK

    @pl.when(pl.program_id(1) == 0)
    def _():
        s_ref[...] = jnp.zeros_like(s_ref)

    lane = lax.broadcasted_iota(jnp.int32, (CHUNK, LANES), 1)
    head0 = lane < HEAD_SIZE
    tt = lax.broadcasted_iota(jnp.int32, (CHUNK, LANES), 0)
    ss = lane % HEAD_SIZE
    r2 = lax.broadcasted_iota(jnp.int32, (LANES, LANES), 0)
    c2 = lax.broadcasted_iota(jnp.int32, (LANES, LANES), 1)
    bd = (r2 // HEAD_SIZE) == (c2 // HEAD_SIZE)
    eye_h = jnp.where(tt == ss, 1.0, 0.0).astype(F32)
    n_double = int(math.log2(CHUNK)) - 1

    def stack2(x):
        zero = jnp.zeros_like(x)
        return jnp.concatenate([jnp.where(head0, x, zero), jnp.where(head0, zero, x)], axis=0)

    fwd_refs = (v_f, at_f, rt_f, bt_f, kt_f, bp_f, kp_f, pe_f)
    bwd_refs = (v_b, at_b, rt_b, bt_b, kt_b, bp_b, kp_b, pe_b)

    def blk(m_h):
        return jnp.where(bd, jnp.concatenate([m_h, m_h], axis=0), jnp.zeros((), m_h.dtype))

    steps = []
    for k in range(n_chunks):
        chains = []
        for d, refs, y_ref, c, strict, incl in ((0, fwd_refs, y_f, k, ss < tt, ss <= tt),
                                                (1, bwd_refs, y_b, n_chunks - 1 - k, ss > tt, ss >= tt)):
            rs = slice(c * CHUNK, (c + 1) * CHUNK)
            for p in range(N_PAIRS):
                ls = slice(p * LANES, (p + 1) * LANES)
                chains.append(dict(d=d, p=p, rs=rs, ls=ls, c=c, refs=refs, y_ref=y_ref, strict=strict, incl=incl))
        steps.append(chains)
    every = [ch for chains in steps for ch in chains]

    for ch in every:
        v_r, at_r, rt_r, bt_r, kt_r, _, _, _ = ch["refs"]
        rs, ls = ch["rs"], ch["ls"]
        hh = _dot_nt(jnp.concatenate([at_r[rs, ls], rt_r[rs, ls]], axis=0),
                     jnp.concatenate([stack2(bt_r[rs, ls]), stack2(kt_r[rs, ls])], axis=0))
        h_ab = jnp.where(ch["strict"], hh[0:CHUNK, 0:LANES], 0.0)
        ch["h_kv"] = jnp.concatenate([jnp.where(ch["strict"], hh[0:CHUNK, LANES:], 0.0),
                                      jnp.where(ch["incl"], hh[CHUNK:, LANES:], 0.0)], axis=0).astype(BF16)
        ch["h_rb"] = jnp.where(ch["incl"], hh[CHUNK:, 0:LANES], 0.0).astype(BF16)
        ch["x"] = h_ab.astype(BF16)
        ch["t"] = eye_h + h_ab

    for ch in every:
        ch["m2"] = _dot(ch["h_kv"], stack2(ch["refs"][0][ch["rs"], ch["ls"]]))
        ch["x"] = _dot(ch["x"], blk(ch["x"])).astype(BF16)

    for k in range(n_double):
        for ch in every:
            x_blk = blk(ch["x"])
            if k + 1 < n_double:
                tx = _dot(jnp.concatenate([ch["t"].astype(BF16), ch["x"]], axis=0), x_blk)
                ch["t"] = ch["t"] + tx[0:CHUNK]
                ch["x"] = tx[CHUNK:].astype(BF16)
            else:
                ch["t"] = (ch["t"] + _dot(ch["t"].astype(BF16), x_blk)).astype(BF16)

    state = {(ch["d"], ch["p"]): s_ref[ch["d"], ch["p"]] for ch in steps[0]}
    for chains in steps:
        for ch in chains:
            _, at_r, rt_r = ch["refs"][:3]
            rs, ls = ch["rs"], ch["ls"]
            st_b = state[ch["d"], ch["p"]].astype(BF16)
            m1 = _dot(jnp.concatenate([at_r[rs, ls], rt_r[rs, ls]], axis=0), st_b)
            ch["w"] = (m1[0:CHUNK] + ch["m2"][0:CHUNK]).astype(BF16)
            ch["y0"] = m1[CHUNK:] + ch["m2"][CHUNK:]
        for ch in chains:
            ch["ub"] = _dot(ch["t"], stack2(ch["w"])).astype(BF16)
        for ch in chains:
            v_r, _, _, _, _, bp_r, kp_r, pe_r = ch["refs"]
            rs, ls = ch["rs"], ch["ls"]
            ch["y_ref"][rs, ls] = ch["y0"] + _dot(ch["h_rb"], stack2(ch["ub"]))
            upd = _dot_tn(jnp.concatenate([bp_r[rs, ls], kp_r[rs, ls]], axis=0),
                          jnp.concatenate([ch["ub"], v_r[rs, ls]], axis=0))
            pend_col = jnp.broadcast_to(pe_r[ch["c"]][:, ls], (LANES, LANES)).T
            state[ch["d"], ch["p"]] = state[ch["d"], ch["p"]] * pend_col + jnp.where(bd, upd, 0.0)

    for (d, p), st in state.items():
        s_ref[d, p] = st


def _scan(v, fwd, bwd, seq_len):
    b = v.shape[0]
    t = min(SCAN_T, seq_len)
    n_tiles = seq_len // t
    nc = t // CHUNK

    def tok(rev):
        if rev:
            return pl.BlockSpec((None, t, RWKV_WIDTH), lambda bi, i: (bi, n_tiles - 1 - i, 0))
        return pl.BlockSpec((None, t, RWKV_WIDTH), lambda bi, i: (bi, i, 0))

    def pe(rev):
        if rev:
            return pl.BlockSpec((None, nc, 1, RWKV_WIDTH), lambda bi, i: (bi, n_tiles - 1 - i, 0, 0))
        return pl.BlockSpec((None, nc, 1, RWKV_WIDTH), lambda bi, i: (bi, i, 0, 0))

    in_specs = [tok(False)] * 7 + [pe(False)] + [tok(True)] * 7 + [pe(True)]
    y_shape = jax.ShapeDtypeStruct((b, seq_len, RWKV_WIDTH), F32)
    return pl.pallas_call(
        _scan_kernel,
        grid=(b, n_tiles),
        in_specs=in_specs,
        out_specs=[tok(False), tok(True)],
        out_shape=[y_shape, y_shape],
        scratch_shapes=[pltpu.VMEM((2, N_PAIRS, LANES, LANES), F32)],
        compiler_params=_params(("parallel", "arbitrary")),
        name="scan",
    )(v, *fwd, v, *bwd)


def _post_kernel(x_ref, yf_ref, yb_ref, bonus_ref, sg_ref, pool_ref, lnw_ref, lnb_ref, gup_ref,
                 wout_ref, postg_ref, o_ref, mix_ref):
    ones_bd = _head_ones2()
    mix_ref[:, 0:POOL_WIDTH] = pool_ref[...]
    inv_n = 1.0 / HEAD_SIZE
    gate = _dot(sg_ref[...], gup_ref[...])
    for p in range(N_PAIRS):
        ls = slice(p * LANES, (p + 1) * LANES)
        y = yf_ref[:, ls] + yb_ref[:, ls]
        mean = _dot2_lhs(y, ones_bd) * inv_n
        yc = y - mean
        var = _dot2_lhs(yc * yc, ones_bd) * inv_n
        yn = yc * lax.rsqrt(var + LNX_EPS) * lnw_ref[:, ls] + lnb_ref[:, ls]
        out = (yn + bonus_ref[:, ls]) * gate[:, ls]
        mix_ref[:, POOL_WIDTH + p * LANES:POOL_WIDTH + (p + 1) * LANES] = out.astype(BF16)
    mixed = _dot(mix_ref[...], wout_ref[...])
    o_ref[...] = x_ref[...] + _rms(mixed, postg_ref[...])


def _post(x, yf, yb, bonus, sg, pool, lnw, lnb, gup, wout, postg):
    n = x.shape[0]
    tm = min(POST_TM, n)
    row = pl.BlockSpec((tm, D_MODEL), lambda i: (i, 0))
    half = pl.BlockSpec((tm, RWKV_WIDTH), lambda i: (i, 0))
    vec_h = pl.BlockSpec((1, RWKV_WIDTH), lambda i: (0, 0))
    return pl.pallas_call(
        _post_kernel,
        grid=(n // tm,),
        in_specs=[row, half, half, half, pl.BlockSpec((tm, GATE_PAD), lambda i: (i, 0)), half, vec_h, vec_h,
                  pl.BlockSpec((GATE_PAD, RWKV_WIDTH), lambda i: (0, 0), pipeline_mode=pl.Buffered(1)),
                  pl.BlockSpec((D_MODEL, D_MODEL), lambda i: (0, 0), pipeline_mode=pl.Buffered(1)),
                  pl.BlockSpec((1, D_MODEL), lambda i: (0, 0))],
        out_specs=row,
        out_shape=jax.ShapeDtypeStruct((n, D_MODEL), F32),
        scratch_shapes=[pltpu.VMEM((tm, D_MODEL), BF16)],
        compiler_params=_params(("parallel",)),
        name="post",
    )(x, yf, yb, bonus, sg, pool, lnw, lnb, gup, wout, postg)


def _ple_kernel(x_ref, p_ref, preg_ref, postg_ref, wgate_ref, wproj_ref, o_ref):
    x = x_ref[...]
    gate = _sigmoid(_dot(_rms(x, preg_ref[...]).astype(BF16), wgate_ref[...]))
    proj = _dot(p_ref[...].astype(BF16), wproj_ref[...])
    o_ref[...] = x + _rms(gate * proj, postg_ref[...])


def _ple(x, p, preg, postg, wgate, wproj):
    n = x.shape[0]
    tm = min(PLE_TM, n)
    row = pl.BlockSpec((tm, D_MODEL), lambda i: (i, 0))
    vec = pl.BlockSpec((1, D_MODEL), lambda i: (0, 0))
    return pl.pallas_call(
        _ple_kernel,
        grid=(n // tm,),
        in_specs=[row, pl.BlockSpec((tm, PLE_DIM), lambda i: (i, 0)), vec, vec,
                  pl.BlockSpec((D_MODEL, D_MODEL), lambda i: (0, 0)),
                  pl.BlockSpec((PLE_DIM, D_MODEL), lambda i: (0, 0))],
        out_specs=row,
        out_shape=jax.ShapeDtypeStruct((n, D_MODEL), F32),
        compiler_params=_params(("parallel",)),
        name="ple",
    )(x, p, preg, postg, wgate, wproj)


def _pad_rows(w, before, total):
    return jnp.pad(w, ((before, total - before - w.shape[0]), (0, 0)))


def _prepare_weights(ffn1_pre_g, ffn1_post_g, ffn1_w_gate, ffn1_w_up, ffn1_w_down,
                     mix_pre_g, mix_post_g, w_in, mu_shift, pool_w, pool_scale,
                     w0_f, w_up_f, a0_f, a_up_f, w0_b, w_up_b, a0_b, a_up_b,
                     g_up, k_k, k_a, r_k, lnx_w, lnx_b, w_out,
                     ffn2_pre_g, ffn2_post_g, ffn2_w_gate, ffn2_w_up, ffn2_w_down,
                     ple_pre_g, ple_post_g, ple_gate_w, ple_proj_w):
    row = lambda a: a.reshape(1, -1)
    bf = lambda a: a.astype(BF16)
    pad_c = Z_COLS - w_in.shape[1]

    def by_pair(a):
        lead = a.shape[:-1]
        rkv = a[..., POOL_WIDTH:LORA_COL].reshape(lead + (3, N_PAIRS, LANES))
        rkv = jnp.swapaxes(rkv, -3, -2).reshape(lead + (3 * RWKV_WIDTH,))
        out = jnp.concatenate([a[..., :POOL_WIDTH], rkv, a[..., LORA_COL:]], axis=-1)
        return jnp.pad(out, [(0, 0)] * len(lead) + [(0, pad_c)])

    mu_full = jnp.concatenate([jnp.zeros((POOL_WIDTH,), F32), mu_shift])
    mixin = (row(mix_pre_g), bf(by_pair(w_in)), row(by_pair(mu_full)[POOL_WIDTH:]), bf(pool_w), row(pool_scale),
             bf(_pad_rows(w_up_f, 0, LANES)), bf(_pad_rows(a_up_f, DECAY_LORA, LANES)),
             bf(_pad_rows(w_up_b, 0, LANES)), bf(_pad_rows(a_up_b, DECAY_LORA, LANES)),
             row(w0_f), row(a0_f), row(w0_b), row(a0_b), row(k_k), row(k_a), row(r_k))
    return dict(
        ffn1=(row(ffn1_pre_g), row(ffn1_post_g), bf(ffn1_w_gate), bf(ffn1_w_up), bf(ffn1_w_down)),
        mixin=mixin,
        post=(row(lnx_w), row(lnx_b), bf(_pad_rows(g_up, 0, GATE_PAD)), bf(w_out), row(mix_post_g)),
        ffn2=(row(ffn2_pre_g), row(ffn2_post_g), bf(ffn2_w_gate), bf(ffn2_w_up), bf(ffn2_w_down)),
        ple=(row(ple_pre_g), row(ple_post_g), bf(ple_gate_w), bf(ple_proj_w)),
    )


def _layer(x, p, wts):
    b, seq_len, _ = x.shape
    n = b * seq_len
    x = _ffn(x.reshape(n, D_MODEL), *wts["ffn1"])
    outs = _mixin(x.reshape(b, seq_len, D_MODEL), wts["mixin"], seq_len)
    pool, v, bonus, sg = outs[:4]
    fwd, bwd = outs[4:11], outs[11:18]
    yf, yb = _scan(v, fwd, bwd, seq_len)
    flat = lambda a: a.reshape(n, a.shape[-1])
    x = _post(x, flat(yf), flat(yb), flat(bonus), flat(sg), flat(pool), *wts["post"])
    x = _ffn(x, *wts["ffn2"])
    x = _ple(x, p.reshape(n, PLE_DIM), *wts["ple"])
    return x.reshape(b, seq_len, D_MODEL)


def kernel(x_prompt, x_sample, p_prompt, p_sample, ffn1_pre_g, ffn1_post_g, ffn1_w_gate, ffn1_w_up, ffn1_w_down, mix_pre_g, mix_post_g, w_in, mu_shift, pool_w, pool_scale, w0_f, w_up_f, a0_f, a_up_f, w0_b, w_up_b, a0_b, a_up_b, g_up, k_k, k_a, r_k, lnx_w, lnx_b, w_out, ffn2_pre_g, ffn2_post_g, ffn2_w_gate, ffn2_w_up, ffn2_w_down, ple_pre_g, ple_post_g, ple_gate_w, ple_proj_w):
    weights = (ffn1_pre_g, ffn1_post_g, ffn1_w_gate, ffn1_w_up, ffn1_w_down, mix_pre_g, mix_post_g, w_in,
               mu_shift, pool_w, pool_scale, w0_f, w_up_f, a0_f, a_up_f, w0_b, w_up_b, a0_b, a_up_b,
               g_up, k_k, k_a, r_k, lnx_w, lnx_b, w_out, ffn2_pre_g, ffn2_post_g, ffn2_w_gate, ffn2_w_up,
               ffn2_w_down, ple_pre_g, ple_post_g, ple_gate_w, ple_proj_w)
    y_prompt, y_sample = x_prompt, x_sample
    for layer in range(ffn1_pre_g.shape[0]):
        wts = _prepare_weights(*(w[layer] for w in weights))
        y_prompt = _layer(y_prompt, p_prompt[layer], wts)
        y_sample = _layer(y_sample, p_sample[layer], wts)
    return (y_prompt, y_sample)
```

```python
import functools
import math

import jax
import jax.numpy as jnp
from jax import lax
from jax.experimental import pallas as pl
from jax.experimental.pallas import tpu as pltpu

F32 = jnp.float32
BF16 = jnp.bfloat16

D_MODEL = 2048
PLE_DIM = 256
POOL_WIDTH = 1024
POOL_WINDOWS = (2, 4, 8, 16)
POOL_GROUP = 256
RWKV_WIDTH = 1024
HEAD_SIZE = 64
DECAY_LORA = 64
ICL_LORA = 64
GATE_LORA = 160
FFN_DIM = 5632
RMS_EPS = 1e-6
LNX_EPS = 64e-5
LOG2E = math.log2(math.e)
DECAY_SCALE2 = math.exp(-0.5) * LOG2E

LANES = 128
SUBLANES = 8
MXU_TILE = 256
N_PAIRS = RWKV_WIDTH // LANES
LORA_PAD = 384
GATE_PAD = LORA_PAD - LANES
LORA_COL = POOL_WIDTH + 3 * RWKV_WIDTH
Z_COLS = LORA_COL + LORA_PAD
CHUNK = 64
HALO = SUBLANES
VMEM_LIMIT = 56 * 1024 * 1024

FFN_TM, FFN_TF = 1024, 512
NORM_ROWS = 16
MIXIN_T = 256
SCAN_T = 256
POST_TM = 512
PLE_TM = 512


def _dot(a, b):
    return jnp.dot(a, b, preferred_element_type=F32)


def _dot_nt(a, b):
    return lax.dot_general(a, b, (((1,), (1,)), ((), ())), preferred_element_type=F32)


def _dot_tn(a, b):
    return lax.dot_general(a, b, (((0,), (0,)), ((), ())), preferred_element_type=F32)


def _split2(x):
    h = x.astype(BF16)
    l = (x - h.astype(F32)).astype(BF16)
    return h, l


def _dot2_lhs(x, w2):
    return _dot(jnp.concatenate(_split2(x), axis=1), w2)


def _sigmoid(x):
    return 1.0 / (1.0 + jnp.exp2(x * (-LOG2E)))


def _rms(x, g):
    ms = jnp.mean(x * x, axis=-1, keepdims=True)
    return x * lax.rsqrt(ms + RMS_EPS) * g


def _residual_norm_inplace(o_ref, x_ref, g):
    for r in range(0, o_ref.shape[0], NORM_ROWS):
        a = o_ref[r:r + NORM_ROWS, :]
        inv = lax.rsqrt(jnp.mean(a * a, axis=-1, keepdims=True) + RMS_EPS)
        o_ref[r:r + NORM_ROWS, :] = x_ref[r:r + NORM_ROWS, :] + a * inv * g


def _head_ones2():
    r = (lax.broadcasted_iota(jnp.int32, (2 * LANES, LANES), 0) % LANES) // HEAD_SIZE
    c = lax.broadcasted_iota(jnp.int32, (2 * LANES, LANES), 1) // HEAD_SIZE
    return jnp.where(r == c, 1.0, 0.0).astype(BF16)


def _params(sem):
    return pltpu.CompilerParams(dimension_semantics=sem, vmem_limit_bytes=VMEM_LIMIT)


def _ffn_kernel(x_ref, pre_g_ref, post_g_ref, wg_ref, wu_ref, wd_ref, o_ref, u_ref, *, n_f):
    j = pl.program_id(1)

    def down_proj():
        u = u_ref[...]
        gate = _dot(u, wg_ref[...])
        up = _dot(u, wu_ref[...])
        h = (gate * _sigmoid(gate)) * up
        return _dot(h.astype(BF16), wd_ref[...])

    @pl.when(j == 0)
    def _():
        u_ref[...] = _rms(x_ref[...], pre_g_ref[...]).astype(BF16)
        o_ref[...] = down_proj()

    @pl.when(j > 0)
    def _():
        o_ref[...] += down_proj()

    @pl.when(j == n_f - 1)
    def _():
        _residual_norm_inplace(o_ref, x_ref, 0.5 * post_g_ref[...])


def _ffn(x, pre_g, post_g, wg, wu, wd):
    n = x.shape[0]
    tm, tf = min(FFN_TM, n), FFN_TF
    grid = (n // tm, FFN_DIM // tf)
    row = pl.BlockSpec((tm, D_MODEL), lambda i, j: (i, 0))
    vec = pl.BlockSpec((1, D_MODEL), lambda i, j: (0, 0))
    return pl.pallas_call(
        functools.partial(_ffn_kernel, n_f=grid[1]),
        grid=grid,
        in_specs=[row, vec, vec,
                  pl.BlockSpec((D_MODEL, tf), lambda i, j: (0, j)),
                  pl.BlockSpec((D_MODEL, tf), lambda i, j: (0, j)),
                  pl.BlockSpec((tf, D_MODEL), lambda i, j: (j, 0))],
        out_specs=row,
        out_shape=jax.ShapeDtypeStruct((n, D_MODEL), F32),
        scratch_shapes=[pltpu.VMEM((tm, D_MODEL), BF16)],
        compiler_params=_params(("parallel", "arbitrary")),
        name="ffn",
    )(x, pre_g, post_g, wg, wu, wd)


def _mixin_kernel(xc_ref, xp_ref, xn_ref, preg_ref, win_ref, mu_ref, poolw_ref, pscale_ref,
                  wup_f_ref, aup_f_ref, wup_b_ref, aup_b_ref,
                  w0f_ref, a0f_ref, w0b_ref, a0b_ref, kk_ref, ka_ref, rk_ref,
                  pool_o, v_o, bonus_o, sg_o,
                  at_f, rt_f, bt_f, kt_f, bp_f, kp_f, pe_f,
                  at_b, rt_b, bt_b, kt_b, bp_b, kp_b, pe_b,
                  u_ref, ext_ref, *, seq_len, n_tiles):
    t_tile = xc_ref.shape[0]
    n_chunks = t_tile // CHUNK
    i = pl.program_id(1)

    x_ext = jnp.concatenate([jnp.where(i > 0, xp_ref[...], 0.0), xc_ref[...],
                             jnp.where(i < n_tiles - 1, xn_ref[...], 0.0)], axis=0)
    u_ref[...] = _rms(x_ext, preg_ref[...]).astype(BF16)

    units = [(gi * POOL_GROUP, POOL_GROUP) for gi in range(len(POOL_WINDOWS))] + [(LORA_COL, LORA_PAD)]
    n_head = len(units)
    for c in range(0, RWKV_WIDTH, MXU_TILE):
        units += [(POOL_WIDTH + s * RWKV_WIDTH + c, MXU_TILE) for s in range(3)]
    issued = [0]

    def ensure(n):
        while issued[0] < min(n, len(units)):
            col, width = units[issued[0]]
            ext_ref[:, col:col + width] = _dot(u_ref[...], win_ref[:, col:col + width])
            issued[0] += 1

    def pair_units(p):
        return n_head + 3 * (p * LANES // MXU_TILE + 1)

    def rows(d, col, width):
        return ext_ref[HALO + d:HALO + d + t_tile, col:col + width]

    ensure(2)

    pos = i * t_tile + lax.broadcasted_iota(jnp.int32, (t_tile, 1), 0)
    for gi, win in enumerate(POOL_WINDOWS):
        half = win // 2
        col = gi * POOL_GROUP
        tot = rows(-half, col, POOL_GROUP)
        for d in range(-half + 1, win - half):
            tot = tot + rows(d, col, POOL_GROUP)
        cnt = (jnp.minimum(pos + (win - half), seq_len) - jnp.maximum(pos - half, 0)).astype(F32)
        pooled = tot / cnt - rows(0, col, POOL_GROUP)
        ensure(gi + 3)
        mixed = _dot(pooled.astype(BF16), poolw_ref[gi])
        pool_o[:, col:col + POOL_GROUP] = (mixed * pscale_ref[:, col:col + POOL_GROUP]).astype(BF16)

    def shifted(col, width):
        mu = mu_ref[:, col - POOL_WIDTH:col - POOL_WIDTH + width]
        return (1.0 - mu) * rows(0, col, width) + (0.5 * mu) * (rows(-1, col, width) + rows(1, col, width))

    ensure(pair_units(0))
    lora = shifted(LORA_COL, LORA_PAD)
    dwa = lora[:, 0:LANES]
    tw = jnp.tanh(dwa).astype(BF16)
    da = dwa.astype(BF16)
    sg_o[...] = _sigmoid(lora[:, LANES:]).astype(BF16)

    rr = lax.broadcasted_iota(jnp.int32, (t_tile, t_tile), 0)
    cc = lax.broadcasted_iota(jnp.int32, (t_tile, t_tile), 1)
    same = (rr // CHUNK) == (cc // CHUNK)
    m_fwd = jnp.where(same & (cc <= rr), 1.0, 0.0).astype(BF16)
    m_bwd = jnp.where(same & (cc >= rr), 1.0, 0.0).astype(BF16)
    ones_bd2 = _head_ones2()

    dirs = ((w0f_ref, a0f_ref, m_fwd, CHUNK - 1, (at_f, rt_f, bt_f, kt_f, bp_f, kp_f, pe_f)),
            (w0b_ref, a0b_ref, m_bwd, 0, (at_b, rt_b, bt_b, kt_b, bp_b, kp_b, pe_b)))

    for p in range(N_PAIRS):
        ls = slice(p * LANES, (p + 1) * LANES)
        ensure(pair_units(p))

        def trickle(p=p):
            ensure(min(issued[0] + 1, pair_units(p + 1)))

        col = POOL_WIDTH + p * LANES
        r = shifted(col, LANES)
        k = shifted(col + RWKV_WIDTH, LANES)
        v = shifted(col + 2 * RWKV_WIDTH, LANES)
        v_o[:, ls] = v.astype(BF16)

        kkr = k * kk_ref[:, ls]
        trickle()
        nrm = jnp.sqrt(_dot2_lhs(kkr * kkr, ones_bd2))
        kk = kkr / jnp.maximum(nrm, 1e-12)
        nkk = -kk

        xw_fb = _dot(tw, jnp.concatenate([wup_f_ref[:, ls], wup_b_ref[:, ls]], axis=1))
        xa_fb = _dot(da, jnp.concatenate([aup_f_ref[:, ls], aup_b_ref[:, ls]], axis=1))

        ksum = None
        for di, (w0_ref, a0_ref, m_dir, end_row, outs) in enumerate(dirs):
            o_at, o_rt, o_bt, o_kt, o_bp, o_kp, o_pe = outs
            dl = slice(di * LANES, (di + 1) * LANES)
            xw = w0_ref[:, ls] + xw_fb[:, dl]
            lw2 = -DECAY_SCALE2 * _sigmoid(xw)
            a = _sigmoid(a0_ref[:, ls] + xa_fb[:, dl])
            kd = k * (1.0 + (a - 1.0) * ka_ref[:, ls])
            ksum = kd if ksum is None else ksum + kd
            beta = kk * a
            trickle()
            lp2 = _dot(m_dir, jnp.concatenate(_split2(lw2), axis=1))
            lp = lp2[:, 0:LANES] + lp2[:, LANES:]
            ends = [lp[c * CHUNK + end_row:c * CHUNK + end_row + 1, :] for c in range(n_chunks)]
            lpe = jnp.concatenate([jnp.broadcast_to(e, (CHUNK, LANES)) for e in ends], axis=0)
            e_inv = jnp.exp2(-lp)
            e_end = jnp.exp2(lpe - lp)
            o_at[:, ls] = (jnp.exp2(lp - lw2) * nkk).astype(BF16)
            o_rt[:, ls] = (jnp.exp2(lp) * r).astype(BF16)
            o_bt[:, ls] = (e_inv * beta).astype(BF16)
            o_kt[:, ls] = (e_inv * kd).astype(BF16)
            o_bp[:, ls] = (e_end * beta).astype(BF16)
            o_kp[:, ls] = (e_end * kd).astype(BF16)
            for c in range(n_chunks):
                o_pe[c, :, ls] = jnp.exp2(ends[c])

        bonus_o[:, ls] = _dot2_lhs(r * ksum * rk_ref[:, ls], ones_bd2) * v


def _mixin(x, wts, seq_len):
    b = x.shape[0]
    t = min(MIXIN_T, seq_len)
    n_tiles = seq_len // t
    n_halo = t // HALO
    last_halo = seq_len // HALO - 1
    cur = pl.BlockSpec((None, t, D_MODEL), lambda bi, i: (bi, i, 0))
    prev = pl.BlockSpec((None, HALO, D_MODEL), lambda bi, i: (bi, jnp.maximum(i * n_halo - 1, 0), 0))
    nxt = pl.BlockSpec((None, HALO, D_MODEL), lambda bi, i: (bi, jnp.minimum((i + 1) * n_halo, last_halo), 0))

    def const(shape):
        nd = len(shape)
        return pl.BlockSpec(shape, lambda bi, i: (0,) * nd, pipeline_mode=pl.Buffered(1))

    tok = pl.BlockSpec((None, t, RWKV_WIDTH), lambda bi, i: (bi, i, 0))
    pe = pl.BlockSpec((None, t // CHUNK, 1, RWKV_WIDTH), lambda bi, i: (bi, i, 0, 0))
    tok_bf = jax.ShapeDtypeStruct((b, seq_len, RWKV_WIDTH), BF16)
    pe_shape = jax.ShapeDtypeStruct((b, seq_len // CHUNK, 1, RWKV_WIDTH), F32)
    vec = const((1, RWKV_WIDTH))
    lora_w = const((LANES, RWKV_WIDTH))
    in_specs = [cur, prev, nxt, const((1, D_MODEL)), const((D_MODEL, Z_COLS)),
                const((1, Z_COLS - POOL_WIDTH)), const((4, POOL_GROUP, POOL_GROUP)), const((1, POOL_WIDTH)),
                lora_w, lora_w, lora_w, lora_w,
                vec, vec, vec, vec, vec, vec, vec]
    out_specs = [pl.BlockSpec((None, t, POOL_WIDTH), lambda bi, i: (bi, i, 0)), tok, tok,
                 pl.BlockSpec((None, t, GATE_PAD), lambda bi, i: (bi, i, 0))] + 2 * ([tok] * 6 + [pe])
    out_shape = [jax.ShapeDtypeStruct((b, seq_len, POOL_WIDTH), BF16), tok_bf,
                 jax.ShapeDtypeStruct((b, seq_len, RWKV_WIDTH), F32),
                 jax.ShapeDtypeStruct((b, seq_len, GATE_PAD), BF16)] + 2 * ([tok_bf] * 6 + [pe_shape])
    return pl.pallas_call(
        functools.partial(_mixin_kernel, seq_len=seq_len, n_tiles=n_tiles),
        grid=(b, n_tiles),
        in_specs=in_specs,
        out_specs=out_specs,
        out_shape=out_shape,
        scratch_shapes=[pltpu.VMEM((t + 2 * HALO, D_MODEL), BF16), pltpu.VMEM((t + 2 * HALO, Z_COLS), F32)],
        compiler_params=_params(("parallel", "arbitrary")),
        name="mixin",
    )(x, x, x, *wts)


def _scan_kernel(v_f, at_f, rt_f, bt_f, kt_f, bp_f, kp_f, pe_f,
                 v_b, at_b, rt_b, bt_b, kt_b, bp_b, kp_b, pe_b,
                 y_f, y_b, s_ref):
    n_chunks = v_f.shape[0] // CHUNK

    @pl.when(pl.program_id(1) == 0)
    def _():
        s_ref[...] = jnp.zeros_like(s_ref)

    lane = lax.broadcasted_iota(jnp.int32, (CHUNK, LANES), 1)
    head0 = lane < HEAD_SIZE
    tt = lax.broadcasted_iota(jnp.int32, (CHUNK, LANES), 0)
    ss = lane % HEAD_SIZE
    r2 = lax.broadcasted_iota(jnp.int32, (LANES, LANES), 0)
    c2 = lax.broadcasted_iota(jnp.int32, (LANES, LANES), 1)
    bd = (r2 // HEAD_SIZE) == (c2 // HEAD_SIZE)
    eye_h = jnp.where(tt == ss, 1.0, 0.0).astype(F32)
    n_double = int(math.log2(CHUNK)) - 1

    def stack2(x):
        zero = jnp.zeros_like(x)
        return jnp.concatenate([jnp.where(head0, x, zero), jnp.where(head0, zero, x)], axis=0)

    fwd_refs = (v_f, at_f, rt_f, bt_f, kt_f, bp_f, kp_f, pe_f)
    bwd_refs = (v_b, at_b, rt_b, bt_b, kt_b, bp_b, kp_b, pe_b)

    def blk(m_h):
        return jnp.where(bd, jnp.concatenate([m_h, m_h], axis=0), jnp.zeros((), m_h.dtype))

    steps = []
    for k in range(n_chunks):
        chains = []
        for d, refs, y_ref, c, strict, incl in ((0, fwd_refs, y_f, k, ss < tt, ss <= tt),
                                                (1, bwd_refs, y_b, n_chunks - 1 - k, ss > tt, ss >= tt)):
            rs = slice(c * CHUNK, (c + 1) * CHUNK)
            for p in range(N_PAIRS):
                ls = slice(p * LANES, (p + 1) * LANES)
                chains.append(dict(d=d, p=p, rs=rs, ls=ls, c=c, refs=refs, y_ref=y_ref, strict=strict, incl=incl))
        steps.append(chains)
    every = [ch for chains in steps for ch in chains]

    for ch in every:
        v_r, at_r, rt_r, bt_r, kt_r, _, _, _ = ch["refs"]
        rs, ls = ch["rs"], ch["ls"]
        hh = _dot_nt(jnp.concatenate([at_r[rs, ls], rt_r[rs, ls]], axis=0),
                     jnp.concatenate([stack2(bt_r[rs, ls]), stack2(kt_r[rs, ls])], axis=0))
        h_ab = jnp.where(ch["strict"], hh[0:CHUNK, 0:LANES], 0.0)
        ch["h_kv"] = jnp.concatenate([jnp.where(ch["strict"], hh[0:CHUNK, LANES:], 0.0),
                                      jnp.where(ch["incl"], hh[CHUNK:, LANES:], 0.0)], axis=0).astype(BF16)
        ch["h_rb"] = jnp.where(ch["incl"], hh[CHUNK:, 0:LANES], 0.0).astype(BF16)
        ch["x"] = h_ab.astype(BF16)
        ch["t"] = eye_h + h_ab

    for ch in every:
        ch["m2"] = _dot(ch["h_kv"], stack2(ch["refs"][0][ch["rs"], ch["ls"]]))
        ch["x"] = _dot(ch["x"], blk(ch["x"])).astype(BF16)

    for k in range(n_double):
        for ch in every:
            x_blk = blk(ch["x"])
            if k + 1 < n_double:
                tx = _dot(jnp.concatenate([ch["t"].astype(BF16), ch["x"]], axis=0), x_blk)
                ch["t"] = ch["t"] + tx[0:CHUNK]
                ch["x"] = tx[CHUNK:].astype(BF16)
            else:
                ch["t"] = (ch["t"] + _dot(ch["t"].astype(BF16), x_blk)).astype(BF16)

    state = {(ch["d"], ch["p"]): s_ref[ch["d"], ch["p"]] for ch in steps[0]}
    for chains in steps:
        for ch in chains:
            _, at_r, rt_r = ch["refs"][:3]
            rs, ls = ch["rs"], ch["ls"]
            st_b = state[ch["d"], ch["p"]].astype(BF16)
            m1 = _dot(jnp.concatenate([at_r[rs, ls], rt_r[rs, ls]], axis=0), st_b)
            ch["w"] = (m1[0:CHUNK] + ch["m2"][0:CHUNK]).astype(BF16)
            ch["y0"] = m1[CHUNK:] + ch["m2"][CHUNK:]
        for ch in chains:
            ch["ub"] = _dot(ch["t"], stack2(ch["w"])).astype(BF16)
        for ch in chains:
            v_r, _, _, _, _, bp_r, kp_r, pe_r = ch["refs"]
            rs, ls = ch["rs"], ch["ls"]
            ch["y_ref"][rs, ls] = ch["y0"] + _dot(ch["h_rb"], stack2(ch["ub"]))
            upd = _dot_tn(jnp.concatenate([bp_r[rs, ls], kp_r[rs, ls]], axis=0),
                          jnp.concatenate([ch["ub"], v_r[rs, ls]], axis=0))
            pend_col = jnp.broadcast_to(pe_r[ch["c"]][:, ls], (LANES, LANES)).T
            state[ch["d"], ch["p"]] = state[ch["d"], ch["p"]] * pend_col + jnp.where(bd, upd, 0.0)

    for (d, p), st in state.items():
        s_ref[d, p] = st


def _scan(v, fwd, bwd, seq_len):
    b = v.shape[0]
    t = min(SCAN_T, seq_len)
    n_tiles = seq_len // t
    nc = t // CHUNK

    def tok(rev):
        if rev:
            return pl.BlockSpec((None, t, RWKV_WIDTH), lambda bi, i: (bi, n_tiles - 1 - i, 0))
        return pl.BlockSpec((None, t, RWKV_WIDTH), lambda bi, i: (bi, i, 0))

    def pe(rev):
        if rev:
            return pl.BlockSpec((None, nc, 1, RWKV_WIDTH), lambda bi, i: (bi, n_tiles - 1 - i, 0, 0))
        return pl.BlockSpec((None, nc, 1, RWKV_WIDTH), lambda bi, i: (bi, i, 0, 0))

    in_specs = [tok(False)] * 7 + [pe(False)] + [tok(True)] * 7 + [pe(True)]
    y_shape = jax.ShapeDtypeStruct((b, seq_len, RWKV_WIDTH), F32)
    return pl.pallas_call(
        _scan_kernel,
        grid=(b, n_tiles),
        in_specs=in_specs,
        out_specs=[tok(False), tok(True)],
        out_shape=[y_shape, y_shape],
        scratch_shapes=[pltpu.VMEM((2, N_PAIRS, LANES, LANES), F32)],
        compiler_params=_params(("parallel", "arbitrary")),
        name="scan",
    )(v, *fwd, v, *bwd)


def _post_kernel(x_ref, yf_ref, yb_ref, bonus_ref, sg_ref, pool_ref, lnw_ref, lnb_ref, gup_ref,
                 wout_ref, postg_ref, o_ref, mix_ref):
    ones_bd = _head_ones2()
    mix_ref[:, 0:POOL_WIDTH] = pool_ref[...]
    inv_n = 1.0 / HEAD_SIZE
    gate = _dot(sg_ref[...], gup_ref[...])
    for p in range(N_PAIRS):
        ls = slice(p * LANES, (p + 1) * LANES)
        y = yf_ref[:, ls] + yb_ref[:, ls]
        mean = _dot2_lhs(y, ones_bd) * inv_n
        yc = y - mean
        var = _dot2_lhs(yc * yc, ones_bd) * inv_n
        yn = yc * lax.rsqrt(var + LNX_EPS) * lnw_ref[:, ls] + lnb_ref[:, ls]
        out = (yn + bonus_ref[:, ls]) * gate[:, ls]
        mix_ref[:, POOL_WIDTH + p * LANES:POOL_WIDTH + (p + 1) * LANES] = out.astype(BF16)
    o_ref[...] = _dot(mix_ref[...], wout_ref[...])
    _residual_norm_inplace(o_ref, x_ref, postg_ref[...])


def _post(x, yf, yb, bonus, sg, pool, lnw, lnb, gup, wout, postg):
    n = x.shape[0]
    tm = min(POST_TM, n)
    row = pl.BlockSpec((tm, D_MODEL), lambda i: (i, 0))
    half = pl.BlockSpec((tm, RWKV_WIDTH), lambda i: (i, 0))
    vec_h = pl.BlockSpec((1, RWKV_WIDTH), lambda i: (0, 0))
    return pl.pallas_call(
        _post_kernel,
        grid=(n // tm,),
        in_specs=[row, half, half, half, pl.BlockSpec((tm, GATE_PAD), lambda i: (i, 0)), half, vec_h, vec_h,
                  pl.BlockSpec((GATE_PAD, RWKV_WIDTH), lambda i: (0, 0), pipeline_mode=pl.Buffered(1)),
                  pl.BlockSpec((D_MODEL, D_MODEL), lambda i: (0, 0), pipeline_mode=pl.Buffered(1)),
                  pl.BlockSpec((1, D_MODEL), lambda i: (0, 0))],
        out_specs=row,
        out_shape=jax.ShapeDtypeStruct((n, D_MODEL), F32),
        scratch_shapes=[pltpu.VMEM((tm, D_MODEL), BF16)],
        compiler_params=_params(("parallel",)),
        name="post",
    )(x, yf, yb, bonus, sg, pool, lnw, lnb, gup, wout, postg)


def _ple_kernel(x_ref, p_ref, preg_ref, postg_ref, wgate_ref, wproj_ref, o_ref):
    gate = _sigmoid(_dot(_rms(x_ref[...], preg_ref[...]).astype(BF16), wgate_ref[...]))
    o_ref[...] = gate * _dot(p_ref[...].astype(BF16), wproj_ref[...])
    _residual_norm_inplace(o_ref, x_ref, postg_ref[...])


def _ple(x, p, preg, postg, wgate, wproj):
    n = x.shape[0]
    tm = min(PLE_TM, n)
    row = pl.BlockSpec((tm, D_MODEL), lambda i: (i, 0))
    vec = pl.BlockSpec((1, D_MODEL), lambda i: (0, 0))
    return pl.pallas_call(
        _ple_kernel,
        grid=(n // tm,),
        in_specs=[row, pl.BlockSpec((tm, PLE_DIM), lambda i: (i, 0)), vec, vec,
                  pl.BlockSpec((D_MODEL, D_MODEL), lambda i: (0, 0)),
                  pl.BlockSpec((PLE_DIM, D_MODEL), lambda i: (0, 0))],
        out_specs=row,
        out_shape=jax.ShapeDtypeStruct((n, D_MODEL), F32),
        compiler_params=_params(("parallel",)),
        name="ple",
    )(x, p, preg, postg, wgate, wproj)


def _pad_rows(w, before, total):
    return jnp.pad(w, ((before, total - before - w.shape[0]), (0, 0)))


def _prepare_weights(ffn1_pre_g, ffn1_post_g, ffn1_w_gate, ffn1_w_up, ffn1_w_down,
                     mix_pre_g, mix_post_g, w_in, mu_shift, pool_w, pool_scale,
                     w0_f, w_up_f, a0_f, a_up_f, w0_b, w_up_b, a0_b, a_up_b,
                     g_up, k_k, k_a, r_k, lnx_w, lnx_b, w_out,
                     ffn2_pre_g, ffn2_post_g, ffn2_w_gate, ffn2_w_up, ffn2_w_down,
                     ple_pre_g, ple_post_g, ple_gate_w, ple_proj_w):
    row = lambda a: a.reshape(1, -1)
    bf = lambda a: a.astype(BF16)
    pad_c = Z_COLS - w_in.shape[1]

    mixin = (row(mix_pre_g), bf(jnp.pad(w_in, ((0, 0), (0, pad_c)))), row(jnp.pad(mu_shift, (0, pad_c))),
             bf(pool_w), row(pool_scale),
             bf(_pad_rows(w_up_f, 0, LANES)), bf(_pad_rows(a_up_f, DECAY_LORA, LANES)),
             bf(_pad_rows(w_up_b, 0, LANES)), bf(_pad_rows(a_up_b, DECAY_LORA, LANES)),
             row(w0_f), row(a0_f), row(w0_b), row(a0_b), row(k_k), row(k_a), row(r_k))
    return dict(
        ffn1=(row(ffn1_pre_g), row(ffn1_post_g), bf(ffn1_w_gate), bf(ffn1_w_up), bf(ffn1_w_down)),
        mixin=mixin,
        post=(row(lnx_w), row(lnx_b), bf(_pad_rows(g_up, 0, GATE_PAD)), bf(w_out), row(mix_post_g)),
        ffn2=(row(ffn2_pre_g), row(ffn2_post_g), bf(ffn2_w_gate), bf(ffn2_w_up), bf(ffn2_w_down)),
        ple=(row(ple_pre_g), row(ple_post_g), bf(ple_gate_w), bf(ple_proj_w)),
    )


def _layer(x, p, wts):
    b, seq_len, _ = x.shape
    n = b * seq_len
    x = _ffn(x.reshape(n, D_MODEL), *wts["ffn1"])
    outs = _mixin(x.reshape(b, seq_len, D_MODEL), wts["mixin"], seq_len)
    pool, v, bonus, sg = outs[:4]
    fwd, bwd = outs[4:11], outs[11:18]
    yf, yb = _scan(v, fwd, bwd, seq_len)
    flat = lambda a: a.reshape(n, a.shape[-1])
    x = _post(x, flat(yf), flat(yb), flat(bonus), flat(sg), flat(pool), *wts["post"])
    x = _ffn(x, *wts["ffn2"])
    x = _ple(x, p.reshape(n, PLE_DIM), *wts["ple"])
    return x.reshape(b, seq_len, D_MODEL)


def kernel(x_prompt, x_sample, p_prompt, p_sample, ffn1_pre_g, ffn1_post_g, ffn1_w_gate, ffn1_w_up, ffn1_w_down, mix_pre_g, mix_post_g, w_in, mu_shift, pool_w, pool_scale, w0_f, w_up_f, a0_f, a_up_f, w0_b, w_up_b, a0_b, a_up_b, g_up, k_k, k_a, r_k, lnx_w, lnx_b, w_out, ffn2_pre_g, ffn2_post_g, ffn2_w_gate, ffn2_w_up, ffn2_w_down, ple_pre_g, ple_post_g, ple_gate_w, ple_proj_w):
    weights = (ffn1_pre_g, ffn1_post_g, ffn1_w_gate, ffn1_w_up, ffn1_w_down, mix_pre_g, mix_post_g, w_in,
               mu_shift, pool_w, pool_scale, w0_f, w_up_f, a0_f, a_up_f, w0_b, w_up_b, a0_b, a_up_b,
               g_up, k_k, k_a, r_k, lnx_w, lnx_b, w_out, ffn2_pre_g, ffn2_post_g, ffn2_w_gate, ffn2_w_up,
               ffn2_w_down, ple_pre_g, ple_post_g, ple_gate_w, ple_proj_w)
    y_prompt, y_sample = x_prompt, x_sample
    for layer in range(ffn1_pre_g.shape[0]):
        wts = _prepare_weights(*(w[layer] for w in weights))
        y_prompt = _layer(y_prompt, p_prompt[layer], wts)
        y_sample = _layer(y_sample, p_sample[layer], wts)
    return (y_prompt, y_sample)
```

```python
import functools
import math

import jax
import jax.numpy as jnp
from jax import lax
from jax.experimental import pallas as pl
from jax.experimental.pallas import tpu as pltpu

F32 = jnp.float32
BF16 = jnp.bfloat16

D_MODEL = 2048
PLE_DIM = 256
POOL_WIDTH = 1024
POOL_WINDOWS = (2, 4, 8, 16)
POOL_GROUP = 256
RWKV_WIDTH = 1024
HEAD_SIZE = 64
DECAY_LORA = 64
ICL_LORA = 64
GATE_LORA = 160
FFN_DIM = 5632
RMS_EPS = 1e-6
LNX_EPS = 64e-5
LOG2E = math.log2(math.e)
DECAY_SCALE2 = math.exp(-0.5) * LOG2E

LANES = 128
SUBLANES = 8
MXU_TILE = 256
N_PAIRS = RWKV_WIDTH // LANES
LORA_PAD = 384
GATE_PAD = LORA_PAD - LANES
LORA_COL = POOL_WIDTH + 3 * RWKV_WIDTH
Z_COLS = LORA_COL + LORA_PAD
CHUNK = 64
HALO = SUBLANES
VMEM_LIMIT = 56 * 1024 * 1024

FFN_TM, FFN_TF = 1024, 512
NORM_ROWS = 16
MIXIN_T = 256
SCAN_T = 256
POST_TM = 512
PLE_TM, PLE_TN = 512, 512


def _dot(a, b):
    return jnp.dot(a, b, preferred_element_type=F32)


def _dot_nt(a, b):
    return lax.dot_general(a, b, (((1,), (1,)), ((), ())), preferred_element_type=F32)


def _dot_tn(a, b):
    return lax.dot_general(a, b, (((0,), (0,)), ((), ())), preferred_element_type=F32)


def _split2(x):
    h = x.astype(BF16)
    l = (x - h.astype(F32)).astype(BF16)
    return h, l


def _sigmoid(x):
    return 1.0 / (1.0 + jnp.exp2(x * (-LOG2E)))


def _rms(x, g):
    ms = jnp.mean(x * x, axis=-1, keepdims=True)
    return x * lax.rsqrt(ms + RMS_EPS) * g


def _residual_norm_inplace(o_ref, x_ref, g):
    for r in range(0, o_ref.shape[0], NORM_ROWS):
        a = o_ref[r:r + NORM_ROWS, :]
        inv = lax.rsqrt(jnp.mean(a * a, axis=-1, keepdims=True) + RMS_EPS)
        o_ref[r:r + NORM_ROWS, :] = x_ref[r:r + NORM_ROWS, :] + a * inv * g


def _head_sum(x):
    head0 = lax.broadcasted_iota(jnp.int32, x.shape, 1) < HEAD_SIZE
    s0 = jnp.sum(jnp.where(head0, x, 0.0), axis=-1, keepdims=True)
    s1 = jnp.sum(jnp.where(head0, 0.0, x), axis=-1, keepdims=True)
    return jnp.where(head0, s0, s1)


def _params(sem):
    return pltpu.CompilerParams(dimension_semantics=sem, vmem_limit_bytes=VMEM_LIMIT)


def _ffn_kernel(x_ref, pre_g_ref, post_g_ref, wgu_ref, wd_ref, o_ref, u_ref, *, n_f):
    j = pl.program_id(1)
    tf = wd_ref.shape[0]

    def down_proj():
        u = u_ref[...]
        gate = _dot(u, wgu_ref[:, 0:tf])
        up = _dot(u, wgu_ref[:, tf:])
        h = (gate * _sigmoid(gate)) * up
        return _dot(h.astype(BF16), wd_ref[...])

    @pl.when(j == 0)
    def _():
        u_ref[...] = _rms(x_ref[...], pre_g_ref[...]).astype(BF16)
        o_ref[...] = down_proj()

    @pl.when(j > 0)
    def _():
        o_ref[...] += down_proj()

    @pl.when(j == n_f - 1)
    def _():
        _residual_norm_inplace(o_ref, x_ref, 0.5 * post_g_ref[...])


def _ffn_weights(w_gate, w_up, w_down):
    n_f = FFN_DIM // FFN_TF
    tiles = lambda w: w.astype(BF16).reshape(D_MODEL, n_f, FFN_TF)
    wgu = jnp.concatenate([tiles(w_gate), tiles(w_up)], axis=2)
    return jnp.swapaxes(wgu, 0, 1), w_down.astype(BF16)


def _ffn(x, pre_g, post_g, wgu, wd):
    n = x.shape[0]
    tm, tf = min(FFN_TM, n), FFN_TF
    grid = (n // tm, FFN_DIM // tf)
    row = pl.BlockSpec((tm, D_MODEL), lambda i, j: (i, 0))
    vec = pl.BlockSpec((1, D_MODEL), lambda i, j: (0, 0))
    return pl.pallas_call(
        functools.partial(_ffn_kernel, n_f=grid[1]),
        grid=grid,
        in_specs=[row, vec, vec,
                  pl.BlockSpec((None, D_MODEL, 2 * tf), lambda i, j: (j, 0, 0)),
                  pl.BlockSpec((tf, D_MODEL), lambda i, j: (j, 0))],
        out_specs=row,
        out_shape=jax.ShapeDtypeStruct((n, D_MODEL), F32),
        scratch_shapes=[pltpu.VMEM((tm, D_MODEL), BF16)],
        compiler_params=_params(("parallel", "arbitrary")),
        name="ffn",
    )(x, pre_g, post_g, wgu, wd)


def _mixin_kernel(xc_ref, xp_ref, xn_ref, preg_ref, win_ref, mu_ref, poolw_ref, pscale_ref,
                  wup_f_ref, aup_f_ref, wup_b_ref, aup_b_ref,
                  w0f_ref, a0f_ref, w0b_ref, a0b_ref, kk_ref, ka_ref, rk_ref,
                  pool_o, v_o, bonus_o, sg_o,
                  at_f, rt_f, bt_f, kt_f, bp_f, kp_f, pe_f,
                  at_b, rt_b, bt_b, kt_b, bp_b, kp_b, pe_b,
                  u_ref, ext_ref, *, seq_len, n_tiles):
    t_tile = xc_ref.shape[0]
    n_chunks = t_tile // CHUNK
    i = pl.program_id(1)

    x_ext = jnp.concatenate([jnp.where(i > 0, xp_ref[...], 0.0), xc_ref[...],
                             jnp.where(i < n_tiles - 1, xn_ref[...], 0.0)], axis=0)
    u_ref[...] = _rms(x_ext, preg_ref[...]).astype(BF16)

    units = [(gi * POOL_GROUP, POOL_GROUP) for gi in range(len(POOL_WINDOWS))] + [(LORA_COL, LORA_PAD)]
    n_head = len(units)
    for c in range(0, RWKV_WIDTH, MXU_TILE):
        units += [(POOL_WIDTH + s * RWKV_WIDTH + c, MXU_TILE) for s in range(3)]
    issued = [0]

    def ensure(n):
        while issued[0] < min(n, len(units)):
            col, width = units[issued[0]]
            ext_ref[:, col:col + width] = _dot(u_ref[...], win_ref[:, col:col + width])
            issued[0] += 1

    def pair_units(p):
        return n_head + 3 * (p * LANES // MXU_TILE + 1)

    def rows(d, col, width):
        return ext_ref[HALO + d:HALO + d + t_tile, col:col + width]

    ensure(2)

    pos = i * t_tile + lax.broadcasted_iota(jnp.int32, (t_tile, 1), 0)
    for gi, win in enumerate(POOL_WINDOWS):
        half = win // 2
        col = gi * POOL_GROUP
        tot = rows(-half, col, POOL_GROUP)
        for d in range(-half + 1, win - half):
            tot = tot + rows(d, col, POOL_GROUP)
        cnt = (jnp.minimum(pos + (win - half), seq_len) - jnp.maximum(pos - half, 0)).astype(F32)
        pooled = tot / cnt - rows(0, col, POOL_GROUP)
        ensure(gi + 3)
        mixed = _dot(pooled.astype(BF16), poolw_ref[gi])
        pool_o[:, col:col + POOL_GROUP] = (mixed * pscale_ref[:, col:col + POOL_GROUP]).astype(BF16)

    def shifted(col, width):
        mu = mu_ref[:, col - POOL_WIDTH:col - POOL_WIDTH + width]
        return (1.0 - mu) * rows(0, col, width) + (0.5 * mu) * (rows(-1, col, width) + rows(1, col, width))

    ensure(pair_units(0))
    lora = shifted(LORA_COL, LORA_PAD)
    dwa = lora[:, 0:LANES]
    tw = jnp.tanh(dwa).astype(BF16)
    da = dwa.astype(BF16)
    sg_o[...] = _sigmoid(lora[:, LANES:]).astype(BF16)

    rr = lax.broadcasted_iota(jnp.int32, (t_tile, t_tile), 0)
    cc = lax.broadcasted_iota(jnp.int32, (t_tile, t_tile), 1)
    same = (rr // CHUNK) == (cc // CHUNK)
    m_fwd = jnp.where(same & (cc <= rr), 1.0, 0.0).astype(BF16)
    m_bwd = jnp.where(same & (cc >= rr), 1.0, 0.0).astype(BF16)

    dirs = ((w0f_ref, a0f_ref, m_fwd, CHUNK - 1, (at_f, rt_f, bt_f, kt_f, bp_f, kp_f, pe_f)),
            (w0b_ref, a0b_ref, m_bwd, 0, (at_b, rt_b, bt_b, kt_b, bp_b, kp_b, pe_b)))

    for p in range(N_PAIRS):
        ls = slice(p * LANES, (p + 1) * LANES)
        ensure(pair_units(p))

        def trickle(p=p):
            ensure(min(issued[0] + 1, pair_units(p + 1)))

        col = POOL_WIDTH + p * LANES
        r = shifted(col, LANES)
        k = shifted(col + RWKV_WIDTH, LANES)
        v = shifted(col + 2 * RWKV_WIDTH, LANES)
        v_o[:, ls] = v.astype(BF16)

        kkr = k * kk_ref[:, ls]
        trickle()
        nrm = jnp.sqrt(_head_sum(kkr * kkr))
        kk = kkr / jnp.maximum(nrm, 1e-12)
        nkk = -kk

        xw_fb = _dot(tw, jnp.concatenate([wup_f_ref[:, ls], wup_b_ref[:, ls]], axis=1))
        xa_fb = _dot(da, jnp.concatenate([aup_f_ref[:, ls], aup_b_ref[:, ls]], axis=1))

        ksum = None
        for di, (w0_ref, a0_ref, m_dir, end_row, outs) in enumerate(dirs):
            o_at, o_rt, o_bt, o_kt, o_bp, o_kp, o_pe = outs
            dl = slice(di * LANES, (di + 1) * LANES)
            xw = w0_ref[:, ls] + xw_fb[:, dl]
            lw2 = -DECAY_SCALE2 * _sigmoid(xw)
            a = _sigmoid(a0_ref[:, ls] + xa_fb[:, dl])
            kd = k * (1.0 + (a - 1.0) * ka_ref[:, ls])
            ksum = kd if ksum is None else ksum + kd
            beta = kk * a
            trickle()
            lp2 = _dot(m_dir, jnp.concatenate(_split2(lw2), axis=1))
            lp = lp2[:, 0:LANES] + lp2[:, LANES:]
            ends = [lp[c * CHUNK + end_row:c * CHUNK + end_row + 1, :] for c in range(n_chunks)]
            lpe = jnp.concatenate([jnp.broadcast_to(e, (CHUNK, LANES)) for e in ends], axis=0)
            e_inv = jnp.exp2(-lp)
            e_end = jnp.exp2(lpe - lp)
            o_at[:, ls] = (jnp.exp2(lp - lw2) * nkk).astype(BF16)
            o_rt[:, ls] = (jnp.exp2(lp) * r).astype(BF16)
            o_bt[:, ls] = (e_inv * beta).astype(BF16)
            o_kt[:, ls] = (e_inv * kd).astype(BF16)
            o_bp[:, ls] = (e_end * beta).astype(BF16)
            o_kp[:, ls] = (e_end * kd).astype(BF16)
            for c in range(n_chunks):
                o_pe[c, :, ls] = jnp.exp2(ends[c])

        bonus_o[:, ls] = _head_sum(r * ksum * rk_ref[:, ls]) * v


def _mixin(x, wts, seq_len):
    b = x.shape[0]
    t = min(MIXIN_T, seq_len)
    n_tiles = seq_len // t
    n_halo = t // HALO
    last_halo = seq_len // HALO - 1
    cur = pl.BlockSpec((None, t, D_MODEL), lambda bi, i: (bi, i, 0))
    prev = pl.BlockSpec((None, HALO, D_MODEL), lambda bi, i: (bi, jnp.maximum(i * n_halo - 1, 0), 0))
    nxt = pl.BlockSpec((None, HALO, D_MODEL), lambda bi, i: (bi, jnp.minimum((i + 1) * n_halo, last_halo), 0))

    def const(shape):
        nd = len(shape)
        return pl.BlockSpec(shape, lambda bi, i: (0,) * nd, pipeline_mode=pl.Buffered(1))

    tok = pl.BlockSpec((None, t, RWKV_WIDTH), lambda bi, i: (bi, i, 0))
    pe = pl.BlockSpec((None, t // CHUNK, 1, RWKV_WIDTH), lambda bi, i: (bi, i, 0, 0))
    tok_bf = jax.ShapeDtypeStruct((b, seq_len, RWKV_WIDTH), BF16)
    pe_shape = jax.ShapeDtypeStruct((b, seq_len // CHUNK, 1, RWKV_WIDTH), F32)
    vec = const((1, RWKV_WIDTH))
    lora_w = const((LANES, RWKV_WIDTH))
    in_specs = [cur, prev, nxt, const((1, D_MODEL)), const((D_MODEL, Z_COLS)),
                const((1, Z_COLS - POOL_WIDTH)), const((4, POOL_GROUP, POOL_GROUP)), const((1, POOL_WIDTH)),
                lora_w, lora_w, lora_w, lora_w,
                vec, vec, vec, vec, vec, vec, vec]
    out_specs = [pl.BlockSpec((None, t, POOL_WIDTH), lambda bi, i: (bi, i, 0)), tok, tok,
                 pl.BlockSpec((None, t, GATE_PAD), lambda bi, i: (bi, i, 0))] + 2 * ([tok] * 6 + [pe])
    out_shape = [jax.ShapeDtypeStruct((b, seq_len, POOL_WIDTH), BF16), tok_bf,
                 jax.ShapeDtypeStruct((b, seq_len, RWKV_WIDTH), F32),
                 jax.ShapeDtypeStruct((b, seq_len, GATE_PAD), BF16)] + 2 * ([tok_bf] * 6 + [pe_shape])
    return pl.pallas_call(
        functools.partial(_mixin_kernel, seq_len=seq_len, n_tiles=n_tiles),
        grid=(b, n_tiles),
        in_specs=in_specs,
        out_specs=out_specs,
        out_shape=out_shape,
        scratch_shapes=[pltpu.VMEM((t + 2 * HALO, D_MODEL), BF16), pltpu.VMEM((t + 2 * HALO, Z_COLS), F32)],
        compiler_params=_params(("parallel", "arbitrary")),
        name="mixin",
    )(x, x, x, *wts)


def _scan_kernel(v_f, at_f, rt_f, bt_f, kt_f, bp_f, kp_f, pe_f,
                 v_b, at_b, rt_b, bt_b, kt_b, bp_b, kp_b, pe_b,
                 y_f, y_b, s_ref):
    n_chunks = v_f.shape[0] // CHUNK

    @pl.when(pl.program_id(1) == 0)
    def _():
        s_ref[...] = jnp.zeros_like(s_ref)

    lane = lax.broadcasted_iota(jnp.int32, (CHUNK, LANES), 1)
    head0 = lane < HEAD_SIZE
    tt = lax.broadcasted_iota(jnp.int32, (CHUNK, LANES), 0)
    ss = lane % HEAD_SIZE
    r2 = lax.broadcasted_iota(jnp.int32, (LANES, LANES), 0)
    c2 = lax.broadcasted_iota(jnp.int32, (LANES, LANES), 1)
    bd = (r2 // HEAD_SIZE) == (c2 // HEAD_SIZE)
    eye_h = jnp.where(tt == ss, 1.0, 0.0).astype(F32)
    n_double = int(math.log2(CHUNK)) - 1

    def stack2(x):
        zero = jnp.zeros_like(x)
        return jnp.concatenate([jnp.where(head0, x, zero), jnp.where(head0, zero, x)], axis=0)

    fwd_refs = (v_f, at_f, rt_f, bt_f, kt_f, bp_f, kp_f, pe_f)
    bwd_refs = (v_b, at_b, rt_b, bt_b, kt_b, bp_b, kp_b, pe_b)

    def blk(m_h):
        return jnp.where(bd, jnp.concatenate([m_h, m_h], axis=0), jnp.zeros((), m_h.dtype))

    steps = []
    for k in range(n_chunks):
        chains = []
        for d, refs, y_ref, c, strict, incl in ((0, fwd_refs, y_f, k, ss < tt, ss <= tt),
                                                (1, bwd_refs, y_b, n_chunks - 1 - k, ss > tt, ss >= tt)):
            rs = slice(c * CHUNK, (c + 1) * CHUNK)
            for p in range(N_PAIRS):
                ls = slice(p * LANES, (p + 1) * LANES)
                chains.append(dict(d=d, p=p, rs=rs, ls=ls, c=c, refs=refs, y_ref=y_ref, strict=strict, incl=incl))
        steps.append(chains)
    every = [ch for chains in steps for ch in chains]

    for ch in every:
        v_r, at_r, rt_r, bt_r, kt_r, _, _, _ = ch["refs"]
        rs, ls = ch["rs"], ch["ls"]
        hh = _dot_nt(jnp.concatenate([at_r[rs, ls], rt_r[rs, ls]], axis=0),
                     jnp.concatenate([stack2(bt_r[rs, ls]), stack2(kt_r[rs, ls])], axis=0))
        h_ab = jnp.where(ch["strict"], hh[0:CHUNK, 0:LANES], 0.0)
        ch["h_kv"] = jnp.concatenate([jnp.where(ch["strict"], hh[0:CHUNK, LANES:], 0.0),
                                      jnp.where(ch["incl"], hh[CHUNK:, LANES:], 0.0)], axis=0).astype(BF16)
        ch["h_rb"] = jnp.where(ch["incl"], hh[CHUNK:, 0:LANES], 0.0).astype(BF16)
        ch["x"] = h_ab.astype(BF16)
        ch["t"] = eye_h + h_ab

    for ch in every:
        ch["m2"] = _dot(ch["h_kv"], stack2(ch["refs"][0][ch["rs"], ch["ls"]]))
        ch["x"] = _dot(ch["x"], blk(ch["x"])).astype(BF16)

    for k in range(n_double):
        for ch in every:
            x_blk = blk(ch["x"])
            if k + 1 < n_double:
                tx = _dot(jnp.concatenate([ch["t"].astype(BF16), ch["x"]], axis=0), x_blk)
                ch["t"] = ch["t"] + tx[0:CHUNK]
                ch["x"] = tx[CHUNK:].astype(BF16)
            else:
                ch["t"] = (ch["t"] + _dot(ch["t"].astype(BF16), x_blk)).astype(BF16)

    state = {(ch["d"], ch["p"]): s_ref[ch["d"], ch["p"]] for ch in steps[0]}
    for chains in steps:
        for ch in chains:
            _, at_r, rt_r = ch["refs"][:3]
            rs, ls = ch["rs"], ch["ls"]
            st_b = state[ch["d"], ch["p"]].astype(BF16)
            m1 = _dot(jnp.concatenate([at_r[rs, ls], rt_r[rs, ls]], axis=0), st_b)
            ch["w"] = (m1[0:CHUNK] + ch["m2"][0:CHUNK]).astype(BF16)
            ch["y0"] = m1[CHUNK:] + ch["m2"][CHUNK:]
        for ch in chains:
            ch["ub"] = _dot(ch["t"], stack2(ch["w"])).astype(BF16)
        for ch in chains:
            v_r, _, _, _, _, bp_r, kp_r, pe_r = ch["refs"]
            rs, ls = ch["rs"], ch["ls"]
            ch["y_ref"][rs, ls] = ch["y0"] + _dot(ch["h_rb"], stack2(ch["ub"]))
            upd = _dot_tn(jnp.concatenate([bp_r[rs, ls], kp_r[rs, ls]], axis=0),
                          jnp.concatenate([ch["ub"], v_r[rs, ls]], axis=0))
            pend_col = jnp.broadcast_to(pe_r[ch["c"]][:, ls], (LANES, LANES)).T
            state[ch["d"], ch["p"]] = state[ch["d"], ch["p"]] * pend_col + jnp.where(bd, upd, 0.0)

    for (d, p), st in state.items():
        s_ref[d, p] = st


def _scan(v, fwd, bwd, seq_len):
    b = v.shape[0]
    t = min(SCAN_T, seq_len)
    n_tiles = seq_len // t
    nc = t // CHUNK

    def tok(rev):
        if rev:
            return pl.BlockSpec((None, t, RWKV_WIDTH), lambda bi, i: (bi, n_tiles - 1 - i, 0))
        return pl.BlockSpec((None, t, RWKV_WIDTH), lambda bi, i: (bi, i, 0))

    def pe(rev):
        if rev:
            return pl.BlockSpec((None, nc, 1, RWKV_WIDTH), lambda bi, i: (bi, n_tiles - 1 - i, 0, 0))
        return pl.BlockSpec((None, nc, 1, RWKV_WIDTH), lambda bi, i: (bi, i, 0, 0))

    in_specs = [tok(False)] * 7 + [pe(False)] + [tok(True)] * 7 + [pe(True)]
    y_shape = jax.ShapeDtypeStruct((b, seq_len, RWKV_WIDTH), F32)
    return pl.pallas_call(
        _scan_kernel,
        grid=(b, n_tiles),
        in_specs=in_specs,
        out_specs=[tok(False), tok(True)],
        out_shape=[y_shape, y_shape],
        scratch_shapes=[pltpu.VMEM((2, N_PAIRS, LANES, LANES), F32)],
        compiler_params=_params(("parallel", "arbitrary")),
        name="scan",
    )(v, *fwd, v, *bwd)


def _post_kernel(x_ref, yf_ref, yb_ref, bonus_ref, sg_ref, pool_ref, lnw_ref, lnb_ref, gup_ref,
                 wout_ref, postg_ref, o_ref, mix_ref):
    inv_n = 1.0 / HEAD_SIZE
    gate = _dot(sg_ref[...], gup_ref[...])
    acc = _dot(pool_ref[...], wout_ref[0:POOL_WIDTH, :])
    for p in range(N_PAIRS):
        ls = slice(p * LANES, (p + 1) * LANES)
        y = yf_ref[:, ls] + yb_ref[:, ls]
        mean = _head_sum(y) * inv_n
        yc = y - mean
        var = _head_sum(yc * yc) * inv_n
        yn = yc * lax.rsqrt(var + LNX_EPS) * lnw_ref[:, ls] + lnb_ref[:, ls]
        out = (yn + bonus_ref[:, ls]) * gate[:, ls]
        mix_ref[:, ls] = out.astype(BF16)
        if (p + 1) * LANES % MXU_TILE == 0:
            ks = slice((p + 1) * LANES - MXU_TILE, (p + 1) * LANES)
            acc = acc + _dot(mix_ref[:, ks], wout_ref[POOL_WIDTH + ks.start:POOL_WIDTH + ks.stop, :])
    o_ref[...] = acc
    _residual_norm_inplace(o_ref, x_ref, postg_ref[...])


def _post(x, yf, yb, bonus, sg, pool, lnw, lnb, gup, wout, postg):
    n = x.shape[0]
    tm = min(POST_TM, n)
    row = pl.BlockSpec((tm, D_MODEL), lambda i: (i, 0))
    half = pl.BlockSpec((tm, RWKV_WIDTH), lambda i: (i, 0))
    vec_h = pl.BlockSpec((1, RWKV_WIDTH), lambda i: (0, 0))
    return pl.pallas_call(
        _post_kernel,
        grid=(n // tm,),
        in_specs=[row, half, half, half, pl.BlockSpec((tm, GATE_PAD), lambda i: (i, 0)), half, vec_h, vec_h,
                  pl.BlockSpec((GATE_PAD, RWKV_WIDTH), lambda i: (0, 0), pipeline_mode=pl.Buffered(1)),
                  pl.BlockSpec((D_MODEL, D_MODEL), lambda i: (0, 0), pipeline_mode=pl.Buffered(1)),
                  pl.BlockSpec((1, D_MODEL), lambda i: (0, 0))],
        out_specs=row,
        out_shape=jax.ShapeDtypeStruct((n, D_MODEL), F32),
        scratch_shapes=[pltpu.VMEM((tm, RWKV_WIDTH), BF16)],
        compiler_params=_params(("parallel",)),
        name="post",
    )(x, yf, yb, bonus, sg, pool, lnw, lnb, gup, wout, postg)


def _ple_kernel(x_ref, p_ref, preg_ref, postg_ref, wgate_ref, wproj_ref, o_ref):
    u = _rms(x_ref[...], preg_ref[...]).astype(BF16)
    p = p_ref[...].astype(BF16)
    for c in range(0, D_MODEL, PLE_TN):
        cs = slice(c, c + PLE_TN)
        o_ref[:, cs] = _sigmoid(_dot(u, wgate_ref[:, cs])) * _dot(p, wproj_ref[:, cs])
    _residual_norm_inplace(o_ref, x_ref, postg_ref[...])


def _ple(x, p, preg, postg, wgate, wproj):
    n = x.shape[0]
    tm = min(PLE_TM, n)
    row = pl.BlockSpec((tm, D_MODEL), lambda i: (i, 0))
    vec = pl.BlockSpec((1, D_MODEL), lambda i: (0, 0))
    return pl.pallas_call(
        _ple_kernel,
        grid=(n // tm,),
        in_specs=[row, pl.BlockSpec((tm, PLE_DIM), lambda i: (i, 0)), vec, vec,
                  pl.BlockSpec((D_MODEL, D_MODEL), lambda i: (0, 0)),
                  pl.BlockSpec((PLE_DIM, D_MODEL), lambda i: (0, 0))],
        out_specs=row,
        out_shape=jax.ShapeDtypeStruct((n, D_MODEL), F32),
        compiler_params=_params(("parallel",)),
        name="ple",
    )(x, p, preg, postg, wgate, wproj)


def _pad_rows(w, before, total):
    return jnp.pad(w, ((before, total - before - w.shape[0]), (0, 0)))


def _prepare_weights(ffn1_pre_g, ffn1_post_g, ffn1_w_gate, ffn1_w_up, ffn1_w_down,
                     mix_pre_g, mix_post_g, w_in, mu_shift, pool_w, pool_scale,
                     w0_f, w_up_f, a0_f, a_up_f, w0_b, w_up_b, a0_b, a_up_b,
                     g_up, k_k, k_a, r_k, lnx_w, lnx_b, w_out,
                     ffn2_pre_g, ffn2_post_g, ffn2_w_gate, ffn2_w_up, ffn2_w_down,
                     ple_pre_g, ple_post_g, ple_gate_w, ple_proj_w):
    row = lambda a: a.reshape(1, -1)
    bf = lambda a: a.astype(BF16)
    pad_c = Z_COLS - w_in.shape[1]

    mixin = (row(mix_pre_g), bf(jnp.pad(w_in, ((0, 0), (0, pad_c)))), row(jnp.pad(mu_shift, (0, pad_c))),
             bf(pool_w), row(pool_scale),
             bf(_pad_rows(w_up_f, 0, LANES)), bf(_pad_rows(a_up_f, DECAY_LORA, LANES)),
             bf(_pad_rows(w_up_b, 0, LANES)), bf(_pad_rows(a_up_b, DECAY_LORA, LANES)),
             row(w0_f), row(a0_f), row(w0_b), row(a0_b), row(k_k), row(k_a), row(r_k))
    return dict(
        ffn1=(row(ffn1_pre_g), row(ffn1_post_g)) + _ffn_weights(ffn1_w_gate, ffn1_w_up, ffn1_w_down),
        mixin=mixin,
        post=(row(lnx_w), row(lnx_b), bf(_pad_rows(g_up, 0, GATE_PAD)), bf(w_out), row(mix_post_g)),
        ffn2=(row(ffn2_pre_g), row(ffn2_post_g)) + _ffn_weights(ffn2_w_gate, ffn2_w_up, ffn2_w_down),
        ple=(row(ple_pre_g), row(ple_post_g), bf(ple_gate_w), bf(ple_proj_w)),
    )


def _layer(x, p, wts):
    b, seq_len, _ = x.shape
    n = b * seq_len
    x = _ffn(x.reshape(n, D_MODEL), *wts["ffn1"])
    outs = _mixin(x.reshape(b, seq_len, D_MODEL), wts["mixin"], seq_len)
    pool, v, bonus, sg = outs[:4]
    fwd, bwd = outs[4:11], outs[11:18]
    yf, yb = _scan(v, fwd, bwd, seq_len)
    flat = lambda a: a.reshape(n, a.shape[-1])
    x = _post(x, flat(yf), flat(yb), flat(bonus), flat(sg), flat(pool), *wts["post"])
    x = _ffn(x, *wts["ffn2"])
    x = _ple(x, p.reshape(n, PLE_DIM), *wts["ple"])
    return x.reshape(b, seq_len, D_MODEL)


def kernel(x_prompt, x_sample, p_prompt, p_sample, ffn1_pre_g, ffn1_post_g, ffn1_w_gate, ffn1_w_up, ffn1_w_down, mix_pre_g, mix_post_g, w_in, mu_shift, pool_w, pool_scale, w0_f, w_up_f, a0_f, a_up_f, w0_b, w_up_b, a0_b, a_up_b, g_up, k_k, k_a, r_k, lnx_w, lnx_b, w_out, ffn2_pre_g, ffn2_post_g, ffn2_w_gate, ffn2_w_up, ffn2_w_down, ple_pre_g, ple_post_g, ple_gate_w, ple_proj_w):
    weights = (ffn1_pre_g, ffn1_post_g, ffn1_w_gate, ffn1_w_up, ffn1_w_down, mix_pre_g, mix_post_g, w_in,
               mu_shift, pool_w, pool_scale, w0_f, w_up_f, a0_f, a_up_f, w0_b, w_up_b, a0_b, a_up_b,
               g_up, k_k, k_a, r_k, lnx_w, lnx_b, w_out, ffn2_pre_g, ffn2_post_g, ffn2_w_gate, ffn2_w_up,
               ffn2_w_down, ple_pre_g, ple_post_g, ple_gate_w, ple_proj_w)
    y_prompt, y_sample = x_prompt, x_sample
    for layer in range(ffn1_pre_g.shape[0]):
        wts = _prepare_weights(*(w[layer] for w in weights))
        y_prompt = _layer(y_prompt, p_prompt[layer], wts)
        y_sample = _layer(y_sample, p_sample[layer], wts)
    return (y_prompt, y_sample)
```

```python
import functools
import math

import jax
import jax.numpy as jnp
from jax import lax
from jax.experimental import pallas as pl
from jax.experimental.pallas import tpu as pltpu

F32 = jnp.float32
BF16 = jnp.bfloat16

D_MODEL = 2048
PLE_DIM = 256
POOL_WIDTH = 1024
POOL_WINDOWS = (2, 4, 8, 16)
POOL_GROUP = 256
RWKV_WIDTH = 1024
HEAD_SIZE = 64
DECAY_LORA = 64
FFN_DIM = 5632
RMS_EPS = 1e-6
LNX_EPS = 64e-5
LOG2E = math.log2(math.e)
DECAY_SCALE2 = math.exp(-0.5) * LOG2E

LANES = 128
SUBLANES = 8
MXU_TILE = 256
N_PAIRS = RWKV_WIDTH // LANES
LORA_PAD = 384
GATE_PAD = LORA_PAD - LANES
LORA_COL = POOL_WIDTH + 3 * RWKV_WIDTH
Z_COLS = LORA_COL + LORA_PAD
CHUNK = 64
HALO = SUBLANES
VMEM_LIMIT = 56 * 1024 * 1024

FFN_TM, FFN_TF = 1024, 512
NORM_ROWS = 16
MIXIN_T = 256
SCAN_T = 256
POST_TM = 512
PLE_TM, PLE_TN = 512, 512


def _dot(a, b):
    return jnp.dot(a, b, preferred_element_type=F32)


def _dot_nt(a, b):
    return lax.dot_general(a, b, (((1,), (1,)), ((), ())), preferred_element_type=F32)


def _dot_tn(a, b):
    return lax.dot_general(a, b, (((0,), (0,)), ((), ())), preferred_element_type=F32)


def _split2(x):
    h = x.astype(BF16)
    l = (x - h.astype(F32)).astype(BF16)
    return h, l


def _sigmoid(x):
    return 1.0 / (1.0 + jnp.exp2(x * (-LOG2E)))


def _rms(x, g):
    ms = jnp.mean(x * x, axis=-1, keepdims=True)
    return x * lax.rsqrt(ms + RMS_EPS) * g


def _residual_norm_inplace(o_ref, x_ref, g):
    for r in range(0, o_ref.shape[0], NORM_ROWS):
        a = o_ref[r:r + NORM_ROWS, :]
        inv = lax.rsqrt(jnp.mean(a * a, axis=-1, keepdims=True) + RMS_EPS)
        o_ref[r:r + NORM_ROWS, :] = x_ref[r:r + NORM_ROWS, :] + a * inv * g


def _head_sum(x):
    head0 = lax.broadcasted_iota(jnp.int32, x.shape, 1) < HEAD_SIZE
    s0 = jnp.sum(jnp.where(head0, x, 0.0), axis=-1, keepdims=True)
    s1 = jnp.sum(jnp.where(head0, 0.0, x), axis=-1, keepdims=True)
    return jnp.where(head0, s0, s1)


def _params(sem):
    return pltpu.CompilerParams(dimension_semantics=sem, vmem_limit_bytes=VMEM_LIMIT)


def _ffn_kernel(x_ref, pre_g_ref, post_g_ref, wg_ref, wu_ref, wd_ref, o_ref, u_ref, *, n_f):
    j = pl.program_id(1)

    def down_proj():
        u = u_ref[...]
        gate = _dot(u, wg_ref[...])
        up = _dot(u, wu_ref[...])
        h = (gate * _sigmoid(gate)) * up
        return _dot(h.astype(BF16), wd_ref[...])

    @pl.when(j == 0)
    def _():
        u_ref[...] = _rms(x_ref[...], pre_g_ref[...]).astype(BF16)
        o_ref[...] = down_proj()

    @pl.when(j > 0)
    def _():
        o_ref[...] += down_proj()

    @pl.when(j == n_f - 1)
    def _():
        _residual_norm_inplace(o_ref, x_ref, 0.5 * post_g_ref[...])


def _ffn(x, pre_g, post_g, wg, wu, wd):
    n = x.shape[0]
    tm, tf = min(FFN_TM, n), FFN_TF
    grid = (n // tm, FFN_DIM // tf)
    row = pl.BlockSpec((tm, D_MODEL), lambda i, j: (i, 0))
    vec = pl.BlockSpec((1, D_MODEL), lambda i, j: (0, 0))
    return pl.pallas_call(
        functools.partial(_ffn_kernel, n_f=grid[1]),
        grid=grid,
        in_specs=[row, vec, vec,
                  pl.BlockSpec((D_MODEL, tf), lambda i, j: (0, j)),
                  pl.BlockSpec((D_MODEL, tf), lambda i, j: (0, j)),
                  pl.BlockSpec((tf, D_MODEL), lambda i, j: (j, 0))],
        out_specs=row,
        out_shape=jax.ShapeDtypeStruct((n, D_MODEL), F32),
        scratch_shapes=[pltpu.VMEM((tm, D_MODEL), BF16)],
        compiler_params=_params(("parallel", "arbitrary")),
        name="ffn",
    )(x, pre_g, post_g, wg, wu, wd)


def _mixin_kernel(xc_ref, xp_ref, xn_ref, preg_ref, win_ref, mu_ref, poolw_ref, pscale_ref,
                  wup_f_ref, aup_f_ref, wup_b_ref, aup_b_ref,
                  w0f_ref, a0f_ref, w0b_ref, a0b_ref, kk_ref, ka_ref, rk_ref,
                  pool_o, v_o, bonus_o, sg_o,
                  at_f, rt_f, bt_f, kt_f, bp_f, kp_f, pe_f,
                  at_b, rt_b, bt_b, kt_b, bp_b, kp_b, pe_b,
                  u_ref, ext_ref, *, seq_len, n_tiles):
    t_tile = xc_ref.shape[0]
    n_chunks = t_tile // CHUNK
    i = pl.program_id(1)

    x_ext = jnp.concatenate([jnp.where(i > 0, xp_ref[...], 0.0), xc_ref[...],
                             jnp.where(i < n_tiles - 1, xn_ref[...], 0.0)], axis=0)
    u_ref[...] = _rms(x_ext, preg_ref[...]).astype(BF16)

    units = [(gi * POOL_GROUP, POOL_GROUP) for gi in range(len(POOL_WINDOWS))] + [(LORA_COL, LORA_PAD)]
    n_head = len(units)
    for c in range(0, RWKV_WIDTH, MXU_TILE):
        units += [(POOL_WIDTH + s * RWKV_WIDTH + c, MXU_TILE) for s in range(3)]
    issued = [0]

    def ensure(n):
        while issued[0] < min(n, len(units)):
            col, width = units[issued[0]]
            ext_ref[:, col:col + width] = _dot(u_ref[...], win_ref[:, col:col + width])
            issued[0] += 1

    def pair_units(p):
        return n_head + 3 * (p * LANES // MXU_TILE + 1)

    def rows(d, col, width):
        return ext_ref[HALO + d:HALO + d + t_tile, col:col + width]

    ensure(2)

    pos = i * t_tile + lax.broadcasted_iota(jnp.int32, (t_tile, 1), 0)
    for gi, win in enumerate(POOL_WINDOWS):
        half = win // 2
        col = gi * POOL_GROUP
        tot = rows(-half, col, POOL_GROUP)
        for d in range(-half + 1, win - half):
            tot = tot + rows(d, col, POOL_GROUP)
        cnt = (jnp.minimum(pos + (win - half), seq_len) - jnp.maximum(pos - half, 0)).astype(F32)
        pooled = tot / cnt - rows(0, col, POOL_GROUP)
        ensure(gi + 3)
        mixed = _dot(pooled.astype(BF16), poolw_ref[gi])
        pool_o[:, col:col + POOL_GROUP] = (mixed * pscale_ref[:, col:col + POOL_GROUP]).astype(BF16)

    def shifted(col, width):
        mu = mu_ref[:, col - POOL_WIDTH:col - POOL_WIDTH + width]
        return (1.0 - mu) * rows(0, col, width) + (0.5 * mu) * (rows(-1, col, width) + rows(1, col, width))

    ensure(pair_units(0))
    lora = shifted(LORA_COL, LORA_PAD)
    dwa = lora[:, 0:LANES]
    tw = jnp.tanh(dwa).astype(BF16)
    da = dwa.astype(BF16)
    sg_o[...] = _sigmoid(lora[:, LANES:]).astype(BF16)

    rr = lax.broadcasted_iota(jnp.int32, (t_tile, t_tile), 0)
    cc = lax.broadcasted_iota(jnp.int32, (t_tile, t_tile), 1)
    same = (rr // CHUNK) == (cc // CHUNK)
    m_fwd = jnp.where(same & (cc <= rr), 1.0, 0.0).astype(BF16)
    m_bwd = jnp.where(same & (cc >= rr), 1.0, 0.0).astype(BF16)

    dirs = ((w0f_ref, a0f_ref, m_fwd, CHUNK - 1, (at_f, rt_f, bt_f, kt_f, bp_f, kp_f, pe_f)),
            (w0b_ref, a0b_ref, m_bwd, 0, (at_b, rt_b, bt_b, kt_b, bp_b, kp_b, pe_b)))

    for p in range(N_PAIRS):
        ls = slice(p * LANES, (p + 1) * LANES)
        ensure(pair_units(p))

        def trickle(p=p):
            ensure(min(issued[0] + 1, pair_units(p + 1)))

        col = POOL_WIDTH + p * LANES
        r = shifted(col, LANES)
        k = shifted(col + RWKV_WIDTH, LANES)
        v = shifted(col + 2 * RWKV_WIDTH, LANES)
        v_o[:, ls] = v.astype(BF16)

        kkr = k * kk_ref[:, ls]
        trickle()
        nrm = jnp.sqrt(_head_sum(kkr * kkr))
        kk = kkr / jnp.maximum(nrm, 1e-12)
        nkk = -kk

        xw_fb = _dot(tw, jnp.concatenate([wup_f_ref[:, ls], wup_b_ref[:, ls]], axis=1))
        xa_fb = _dot(da, jnp.concatenate([aup_f_ref[:, ls], aup_b_ref[:, ls]], axis=1))

        ksum = None
        for di, (w0_ref, a0_ref, m_dir, end_row, outs) in enumerate(dirs):
            o_at, o_rt, o_bt, o_kt, o_bp, o_kp, o_pe = outs
            dl = slice(di * LANES, (di + 1) * LANES)
            xw = w0_ref[:, ls] + xw_fb[:, dl]
            lw2 = -DECAY_SCALE2 * _sigmoid(xw)
            a = _sigmoid(a0_ref[:, ls] + xa_fb[:, dl])
            kd = k * (1.0 + (a - 1.0) * ka_ref[:, ls])
            ksum = kd if ksum is None else ksum + kd
            beta = kk * a
            trickle()
            lp2 = _dot(m_dir, jnp.concatenate(_split2(lw2), axis=1))
            lp = lp2[:, 0:LANES] + lp2[:, LANES:]
            ends = [lp[c * CHUNK + end_row:c * CHUNK + end_row + 1, :] for c in range(n_chunks)]
            lpe = jnp.concatenate([jnp.broadcast_to(e, (CHUNK, LANES)) for e in ends], axis=0)
            e_inv = jnp.exp2(-lp)
            e_end = jnp.exp2(lpe - lp)
            o_at[:, ls] = (jnp.exp2(lp - lw2) * nkk).astype(BF16)
            o_rt[:, ls] = (jnp.exp2(lp) * r).astype(BF16)
            o_bt[:, ls] = (e_inv * beta).astype(BF16)
            o_kt[:, ls] = (e_inv * kd).astype(BF16)
            o_bp[:, ls] = (e_end * beta).astype(BF16)
            o_kp[:, ls] = (e_end * kd).astype(BF16)
            for c in range(n_chunks):
                o_pe[c, :, ls] = jnp.exp2(ends[c])

        bonus_o[:, ls] = _head_sum(r * ksum * rk_ref[:, ls]) * v


def _mixin(x, wts, seq_len):
    b = x.shape[0]
    t = min(MIXIN_T, seq_len)
    n_tiles = seq_len // t
    n_halo = t // HALO
    last_halo = seq_len // HALO - 1
    cur = pl.BlockSpec((None, t, D_MODEL), lambda bi, i: (bi, i, 0))
    prev = pl.BlockSpec((None, HALO, D_MODEL), lambda bi, i: (bi, jnp.maximum(i * n_halo - 1, 0), 0))
    nxt = pl.BlockSpec((None, HALO, D_MODEL), lambda bi, i: (bi, jnp.minimum((i + 1) * n_halo, last_halo), 0))

    def const(shape):
        nd = len(shape)
        return pl.BlockSpec(shape, lambda bi, i: (0,) * nd, pipeline_mode=pl.Buffered(1))

    tok = pl.BlockSpec((None, t, RWKV_WIDTH), lambda bi, i: (bi, i, 0))
    pe = pl.BlockSpec((None, t // CHUNK, 1, RWKV_WIDTH), lambda bi, i: (bi, i, 0, 0))
    tok_bf = jax.ShapeDtypeStruct((b, seq_len, RWKV_WIDTH), BF16)
    pe_shape = jax.ShapeDtypeStruct((b, seq_len // CHUNK, 1, RWKV_WIDTH), F32)
    vec = const((1, RWKV_WIDTH))
    lora_w = const((LANES, RWKV_WIDTH))
    in_specs = [cur, prev, nxt, const((1, D_MODEL)), const((D_MODEL, Z_COLS)),
                const((1, Z_COLS - POOL_WIDTH)), const((4, POOL_GROUP, POOL_GROUP)), const((1, POOL_WIDTH)),
                lora_w, lora_w, lora_w, lora_w,
                vec, vec, vec, vec, vec, vec, vec]
    out_specs = [pl.BlockSpec((None, t, POOL_WIDTH), lambda bi, i: (bi, i, 0)), tok, tok,
                 pl.BlockSpec((None, t, GATE_PAD), lambda bi, i: (bi, i, 0))] + 2 * ([tok] * 6 + [pe])
    out_shape = [jax.ShapeDtypeStruct((b, seq_len, POOL_WIDTH), BF16), tok_bf,
                 jax.ShapeDtypeStruct((b, seq_len, RWKV_WIDTH), F32),
                 jax.ShapeDtypeStruct((b, seq_len, GATE_PAD), BF16)] + 2 * ([tok_bf] * 6 + [pe_shape])
    return pl.pallas_call(
        functools.partial(_mixin_kernel, seq_len=seq_len, n_tiles=n_tiles),
        grid=(b, n_tiles),
        in_specs=in_specs,
        out_specs=out_specs,
        out_shape=out_shape,
        scratch_shapes=[pltpu.VMEM((t + 2 * HALO, D_MODEL), BF16), pltpu.VMEM((t + 2 * HALO, Z_COLS), F32)],
        compiler_params=_params(("parallel", "arbitrary")),
        name="mixin",
    )(x, x, x, *wts)


def _scan_kernel(v_f, at_f, rt_f, bt_f, kt_f, bp_f, kp_f, pe_f,
                 v_b, at_b, rt_b, bt_b, kt_b, bp_b, kp_b, pe_b,
                 y_f, y_b, s_ref):
    n_chunks = v_f.shape[0] // CHUNK

    @pl.when(pl.program_id(1) == 0)
    def _():
        s_ref[...] = jnp.zeros_like(s_ref)

    lane = lax.broadcasted_iota(jnp.int32, (CHUNK, LANES), 1)
    head0 = lane < HEAD_SIZE
    tt = lax.broadcasted_iota(jnp.int32, (CHUNK, LANES), 0)
    ss = lane % HEAD_SIZE
    r2 = lax.broadcasted_iota(jnp.int32, (LANES, LANES), 0)
    c2 = lax.broadcasted_iota(jnp.int32, (LANES, LANES), 1)
    bd = (r2 // HEAD_SIZE) == (c2 // HEAD_SIZE)
    eye_h = jnp.where(tt == ss, 1.0, 0.0).astype(F32)
    n_double = int(math.log2(CHUNK)) - 1

    def stack2(x):
        zero = jnp.zeros_like(x)
        return jnp.concatenate([jnp.where(head0, x, zero), jnp.where(head0, zero, x)], axis=0)

    fwd_refs = (v_f, at_f, rt_f, bt_f, kt_f, bp_f, kp_f, pe_f)
    bwd_refs = (v_b, at_b, rt_b, bt_b, kt_b, bp_b, kp_b, pe_b)

    def blk(m_h):
        return jnp.where(bd, jnp.concatenate([m_h, m_h], axis=0), jnp.zeros((), m_h.dtype))

    steps = []
    for k in range(n_chunks):
        chains = []
        for d, refs, y_ref, c, strict, incl in ((0, fwd_refs, y_f, k, ss < tt, ss <= tt),
                                                (1, bwd_refs, y_b, n_chunks - 1 - k, ss > tt, ss >= tt)):
            rs = slice(c * CHUNK, (c + 1) * CHUNK)
            for p in range(N_PAIRS):
                ls = slice(p * LANES, (p + 1) * LANES)
                chains.append(dict(d=d, p=p, rs=rs, ls=ls, c=c, refs=refs, y_ref=y_ref, strict=strict, incl=incl))
        steps.append(chains)
    every = [ch for chains in steps for ch in chains]

    for ch in every:
        v_r, at_r, rt_r, bt_r, kt_r, _, _, _ = ch["refs"]
        rs, ls = ch["rs"], ch["ls"]
        hh = _dot_nt(jnp.concatenate([at_r[rs, ls], rt_r[rs, ls]], axis=0),
                     jnp.concatenate([stack2(bt_r[rs, ls]), stack2(kt_r[rs, ls])], axis=0))
        h_ab = jnp.where(ch["strict"], hh[0:CHUNK, 0:LANES], 0.0)
        ch["h_kv"] = jnp.concatenate([jnp.where(ch["strict"], hh[0:CHUNK, LANES:], 0.0),
                                      jnp.where(ch["incl"], hh[CHUNK:, LANES:], 0.0)], axis=0).astype(BF16)
        ch["h_rb"] = jnp.where(ch["incl"], hh[CHUNK:, 0:LANES], 0.0).astype(BF16)
        ch["x"] = h_ab.astype(BF16)
        ch["t"] = eye_h + h_ab

    for ch in every:
        ch["m2"] = _dot(ch["h_kv"], stack2(ch["refs"][0][ch["rs"], ch["ls"]]))
        ch["x"] = _dot(ch["x"], blk(ch["x"])).astype(BF16)

    for k in range(n_double):
        for ch in every:
            x_blk = blk(ch["x"])
            if k + 1 < n_double:
                tx = _dot(jnp.concatenate([ch["t"].astype(BF16), ch["x"]], axis=0), x_blk)
                ch["t"] = ch["t"] + tx[0:CHUNK]
                ch["x"] = tx[CHUNK:].astype(BF16)
            else:
                ch["t"] = (ch["t"] + _dot(ch["t"].astype(BF16), x_blk)).astype(BF16)

    state = {(ch["d"], ch["p"]): s_ref[ch["d"], ch["p"]] for ch in steps[0]}
    for chains in steps:
        for ch in chains:
            _, at_r, rt_r = ch["refs"][:3]
            rs, ls = ch["rs"], ch["ls"]
            st_b = state[ch["d"], ch["p"]].astype(BF16)
            m1 = _dot(jnp.concatenate([at_r[rs, ls], rt_r[rs, ls]], axis=0), st_b)
            ch["w"] = (m1[0:CHUNK] + ch["m2"][0:CHUNK]).astype(BF16)
            ch["y0"] = m1[CHUNK:] + ch["m2"][CHUNK:]
        for ch in chains:
            ch["ub"] = _dot(ch["t"], stack2(ch["w"])).astype(BF16)
        for ch in chains:
            v_r, _, _, _, _, bp_r, kp_r, pe_r = ch["refs"]
            rs, ls = ch["rs"], ch["ls"]
            ch["y_ref"][rs, ls] = ch["y0"] + _dot(ch["h_rb"], stack2(ch["ub"]))
            upd = _dot_tn(jnp.concatenate([bp_r[rs, ls], kp_r[rs, ls]], axis=0),
                          jnp.concatenate([ch["ub"], v_r[rs, ls]], axis=0))
            pend_col = jnp.broadcast_to(pe_r[ch["c"]][:, ls], (LANES, LANES)).T
            state[ch["d"], ch["p"]] = state[ch["d"], ch["p"]] * pend_col + jnp.where(bd, upd, 0.0)

    for (d, p), st in state.items():
        s_ref[d, p] = st


def _scan(v, fwd, bwd, seq_len):
    b = v.shape[0]
    t = min(SCAN_T, seq_len)
    n_tiles = seq_len // t
    nc = t // CHUNK

    def tok(rev):
        if rev:
            return pl.BlockSpec((None, t, RWKV_WIDTH), lambda bi, i: (bi, n_tiles - 1 - i, 0))
        return pl.BlockSpec((None, t, RWKV_WIDTH), lambda bi, i: (bi, i, 0))

    def pe(rev):
        if rev:
            return pl.BlockSpec((None, nc, 1, RWKV_WIDTH), lambda bi, i: (bi, n_tiles - 1 - i, 0, 0))
        return pl.BlockSpec((None, nc, 1, RWKV_WIDTH), lambda bi, i: (bi, i, 0, 0))

    in_specs = [tok(False)] * 7 + [pe(False)] + [tok(True)] * 7 + [pe(True)]
    y_shape = jax.ShapeDtypeStruct((b, seq_len, RWKV_WIDTH), F32)
    return pl.pallas_call(
        _scan_kernel,
        grid=(b, n_tiles),
        in_specs=in_specs,
        out_specs=[tok(False), tok(True)],
        out_shape=[y_shape, y_shape],
        scratch_shapes=[pltpu.VMEM((2, N_PAIRS, LANES, LANES), F32)],
        compiler_params=_params(("parallel", "arbitrary")),
        name="scan",
    )(v, *fwd, v, *bwd)


def _post_kernel(x_ref, yf_ref, yb_ref, bonus_ref, sg_ref, pool_ref, lnw_ref, lnb_ref, gup_ref,
                 wout_ref, postg_ref, o_ref, mix_ref):
    inv_n = 1.0 / HEAD_SIZE
    gate = _dot(sg_ref[...], gup_ref[...])
    acc = _dot(pool_ref[...], wout_ref[0:POOL_WIDTH, :])
    for p in range(N_PAIRS):
        ls = slice(p * LANES, (p + 1) * LANES)
        y = yf_ref[:, ls] + yb_ref[:, ls]
        mean = _head_sum(y) * inv_n
        yc = y - mean
        var = _head_sum(yc * yc) * inv_n
        yn = yc * lax.rsqrt(var + LNX_EPS) * lnw_ref[:, ls] + lnb_ref[:, ls]
        out = (yn + bonus_ref[:, ls]) * gate[:, ls]
        mix_ref[:, ls] = out.astype(BF16)
        if (p + 1) * LANES % MXU_TILE == 0:
            ks = slice((p + 1) * LANES - MXU_TILE, (p + 1) * LANES)
            acc = acc + _dot(mix_ref[:, ks], wout_ref[POOL_WIDTH + ks.start:POOL_WIDTH + ks.stop, :])
    o_ref[...] = acc
    _residual_norm_inplace(o_ref, x_ref, postg_ref[...])


def _post(x, yf, yb, bonus, sg, pool, lnw, lnb, gup, wout, postg):
    n = x.shape[0]
    tm = min(POST_TM, n)
    row = pl.BlockSpec((tm, D_MODEL), lambda i: (i, 0))
    half = pl.BlockSpec((tm, RWKV_WIDTH), lambda i: (i, 0))
    vec_h = pl.BlockSpec((1, RWKV_WIDTH), lambda i: (0, 0))
    return pl.pallas_call(
        _post_kernel,
        grid=(n // tm,),
        in_specs=[row, half, half, half, pl.BlockSpec((tm, GATE_PAD), lambda i: (i, 0)), half, vec_h, vec_h,
                  pl.BlockSpec((GATE_PAD, RWKV_WIDTH), lambda i: (0, 0), pipeline_mode=pl.Buffered(1)),
                  pl.BlockSpec((D_MODEL, D_MODEL), lambda i: (0, 0), pipeline_mode=pl.Buffered(1)),
                  pl.BlockSpec((1, D_MODEL), lambda i: (0, 0))],
        out_specs=row,
        out_shape=jax.ShapeDtypeStruct((n, D_MODEL), F32),
        scratch_shapes=[pltpu.VMEM((tm, RWKV_WIDTH), BF16)],
        compiler_params=_params(("parallel",)),
        name="post",
    )(x, yf, yb, bonus, sg, pool, lnw, lnb, gup, wout, postg)


def _ple_kernel(x_ref, p_ref, preg_ref, postg_ref, wgate_ref, wproj_ref, o_ref):
    u = _rms(x_ref[...], preg_ref[...]).astype(BF16)
    p = p_ref[...].astype(BF16)
    for c in range(0, D_MODEL, PLE_TN):
        cs = slice(c, c + PLE_TN)
        o_ref[:, cs] = _sigmoid(_dot(u, wgate_ref[:, cs])) * _dot(p, wproj_ref[:, cs])
    _residual_norm_inplace(o_ref, x_ref, postg_ref[...])


def _ple(x, p, preg, postg, wgate, wproj):
    n = x.shape[0]
    tm = min(PLE_TM, n)
    row = pl.BlockSpec((tm, D_MODEL), lambda i: (i, 0))
    vec = pl.BlockSpec((1, D_MODEL), lambda i: (0, 0))
    return pl.pallas_call(
        _ple_kernel,
        grid=(n // tm,),
        in_specs=[row, pl.BlockSpec((tm, PLE_DIM), lambda i: (i, 0)), vec, vec,
                  pl.BlockSpec((D_MODEL, D_MODEL), lambda i: (0, 0)),
                  pl.BlockSpec((PLE_DIM, D_MODEL), lambda i: (0, 0))],
        out_specs=row,
        out_shape=jax.ShapeDtypeStruct((n, D_MODEL), F32),
        compiler_params=_params(("parallel",)),
        name="ple",
    )(x, p, preg, postg, wgate, wproj)


def _pad_rows(w, before, total):
    return jnp.pad(w, ((before, total - before - w.shape[0]), (0, 0)))


def _prepare_weights(ffn1_pre_g, ffn1_post_g, ffn1_w_gate, ffn1_w_up, ffn1_w_down,
                     mix_pre_g, mix_post_g, w_in, mu_shift, pool_w, pool_scale,
                     w0_f, w_up_f, a0_f, a_up_f, w0_b, w_up_b, a0_b, a_up_b,
                     g_up, k_k, k_a, r_k, lnx_w, lnx_b, w_out,
                     ffn2_pre_g, ffn2_post_g, ffn2_w_gate, ffn2_w_up, ffn2_w_down,
                     ple_pre_g, ple_post_g, ple_gate_w, ple_proj_w):
    row = lambda a: a.reshape(1, -1)
    bf = lambda a: a.astype(BF16)
    pad_c = Z_COLS - w_in.shape[1]

    mixin = (row(mix_pre_g), bf(jnp.pad(w_in, ((0, 0), (0, pad_c)))), row(jnp.pad(mu_shift, (0, pad_c))),
             bf(pool_w), row(pool_scale),
             bf(_pad_rows(w_up_f, 0, LANES)), bf(_pad_rows(a_up_f, DECAY_LORA, LANES)),
             bf(_pad_rows(w_up_b, 0, LANES)), bf(_pad_rows(a_up_b, DECAY_LORA, LANES)),
             row(w0_f), row(a0_f), row(w0_b), row(a0_b), row(k_k), row(k_a), row(r_k))
    return dict(
        ffn1=(row(ffn1_pre_g), row(ffn1_post_g), bf(ffn1_w_gate), bf(ffn1_w_up), bf(ffn1_w_down)),
        mixin=mixin,
        post=(row(lnx_w), row(lnx_b), bf(_pad_rows(g_up, 0, GATE_PAD)), bf(w_out), row(mix_post_g)),
        ffn2=(row(ffn2_pre_g), row(ffn2_post_g), bf(ffn2_w_gate), bf(ffn2_w_up), bf(ffn2_w_down)),
        ple=(row(ple_pre_g), row(ple_post_g), bf(ple_gate_w), bf(ple_proj_w)),
    )


def _layer(x, p, wts):
    b, seq_len, _ = x.shape
    n = b * seq_len
    x = _ffn(x.reshape(n, D_MODEL), *wts["ffn1"])
    outs = _mixin(x.reshape(b, seq_len, D_MODEL), wts["mixin"], seq_len)
    pool, v, bonus, sg = outs[:4]
    fwd, bwd = outs[4:11], outs[11:18]
    yf, yb = _scan(v, fwd, bwd, seq_len)
    flat = lambda a: a.reshape(n, a.shape[-1])
    x = _post(x, flat(yf), flat(yb), flat(bonus), flat(sg), flat(pool), *wts["post"])
    x = _ffn(x, *wts["ffn2"])
    x = _ple(x, p.reshape(n, PLE_DIM), *wts["ple"])
    return x.reshape(b, seq_len, D_MODEL)


def kernel(x_prompt, x_sample, p_prompt, p_sample, ffn1_pre_g, ffn1_post_g, ffn1_w_gate, ffn1_w_up, ffn1_w_down, mix_pre_g, mix_post_g, w_in, mu_shift, pool_w, pool_scale, w0_f, w_up_f, a0_f, a_up_f, w0_b, w_up_b, a0_b, a_up_b, g_up, k_k, k_a, r_k, lnx_w, lnx_b, w_out, ffn2_pre_g, ffn2_post_g, ffn2_w_gate, ffn2_w_up, ffn2_w_down, ple_pre_g, ple_post_g, ple_gate_w, ple_proj_w):
    weights = (ffn1_pre_g, ffn1_post_g, ffn1_w_gate, ffn1_w_up, ffn1_w_down, mix_pre_g, mix_post_g, w_in,
               mu_shift, pool_w, pool_scale, w0_f, w_up_f, a0_f, a_up_f, w0_b, w_up_b, a0_b, a_up_b,
               g_up, k_k, k_a, r_k, lnx_w, lnx_b, w_out, ffn2_pre_g, ffn2_post_g, ffn2_w_gate, ffn2_w_up,
               ffn2_w_down, ple_pre_g, ple_post_g, ple_gate_w, ple_proj_w)
    y_prompt, y_sample = x_prompt, x_sample
    for layer in range(ffn1_pre_g.shape[0]):
        wts = _prepare_weights(*(w[layer] for w in weights))
        y_prompt = _layer(y_prompt, p_prompt[layer], wts)
        y_sample = _layer(y_sample, p_sample[layer], wts)
    return (y_prompt, y_sample)
```

```python
import functools
import math

import jax
import jax.numpy as jnp
from jax import lax
from jax.experimental import pallas as pl
from jax.experimental.pallas import tpu as pltpu

F32 = jnp.float32
BF16 = jnp.bfloat16

D_MODEL = 2048
PLE_DIM = 256
POOL_WIDTH = 1024
POOL_WINDOWS = (2, 4, 8, 16)
POOL_GROUP = 256
RWKV_WIDTH = 1024
HEAD_SIZE = 64
DECAY_LORA = 64
FFN_DIM = 5632
RMS_EPS = 1e-6
LNX_EPS = 64e-5
LOG2E = math.log2(math.e)
DECAY_SCALE2 = math.exp(-0.5) * LOG2E

LANES = 128
SUBLANES = 8
MXU_TILE = 256
N_PAIRS = RWKV_WIDTH // LANES
LORA_PAD = 384
GATE_PAD = LORA_PAD - LANES
LORA_COL = POOL_WIDTH + 3 * RWKV_WIDTH
Z_COLS = LORA_COL + LORA_PAD
CHUNK = 64
HALO = SUBLANES
EXT_ROW_MULT = 32
MIXIN_VECS = (("pre_g", D_MODEL), ("mu", Z_COLS - POOL_WIDTH), ("pool_scale", POOL_WIDTH),
              ("w0_f", RWKV_WIDTH), ("a0_f", RWKV_WIDTH), ("w0_b", RWKV_WIDTH), ("a0_b", RWKV_WIDTH),
              ("k_k", RWKV_WIDTH), ("k_a", RWKV_WIDTH), ("r_k", RWKV_WIDTH))
MIXIN_VEC_PAD = 16384
VMEM_LIMIT = 56 * 1024 * 1024

FFN_TM, FFN_TF = 1024, 512
NORM_ROWS = 16
MIXIN_T = 256
SCAN_T = 256
POST_TM = 512
PLE_TM, PLE_TN = 512, 512


def _dot(a, b):
    return jnp.dot(a, b, preferred_element_type=F32)


def _dot_nt(a, b):
    return lax.dot_general(a, b, (((1,), (1,)), ((), ())), preferred_element_type=F32)


def _dot_tn(a, b):
    return lax.dot_general(a, b, (((0,), (0,)), ((), ())), preferred_element_type=F32)


def _split2(x):
    h = x.astype(BF16)
    l = (x - h.astype(F32)).astype(BF16)
    return h, l


def _sigmoid(x):
    return 1.0 / (1.0 + jnp.exp2(x * (-LOG2E)))


def _rms(x, g):
    ms = jnp.mean(x * x, axis=-1, keepdims=True)
    return x * lax.rsqrt(ms + RMS_EPS) * g


def _residual_norm_inplace(o_ref, x_ref, g):
    for r in range(0, o_ref.shape[0], NORM_ROWS):
        a = o_ref[r:r + NORM_ROWS, :]
        inv = lax.rsqrt(jnp.mean(a * a, axis=-1, keepdims=True) + RMS_EPS)
        o_ref[r:r + NORM_ROWS, :] = x_ref[r:r + NORM_ROWS, :] + a * inv * g


def _head_sum(x):
    head0 = lax.broadcasted_iota(jnp.int32, x.shape, 1) < HEAD_SIZE
    s0 = jnp.sum(jnp.where(head0, x, 0.0), axis=-1, keepdims=True)
    s1 = jnp.sum(jnp.where(head0, 0.0, x), axis=-1, keepdims=True)
    return jnp.where(head0, s0, s1)


def _params(sem):
    return pltpu.CompilerParams(dimension_semantics=sem, vmem_limit_bytes=VMEM_LIMIT)


def _ffn_kernel(x_ref, pre_g_ref, post_g_ref, wg_ref, wu_ref, wd_ref, o_ref, u_ref, *, n_f):
    j = pl.program_id(1)

    def down_proj():
        u = u_ref[...]
        gate = _dot(u, wg_ref[...])
        up = _dot(u, wu_ref[...])
        h = (gate * _sigmoid(gate)) * up
        return _dot(h.astype(BF16), wd_ref[...])

    @pl.when(j == 0)
    def _():
        u_ref[...] = _rms(x_ref[...], pre_g_ref[...]).astype(BF16)
        o_ref[...] = down_proj()

    @pl.when(j > 0)
    def _():
        o_ref[...] += down_proj()

    @pl.when(j == n_f - 1)
    def _():
        _residual_norm_inplace(o_ref, x_ref, 0.5 * post_g_ref[...])


def _ffn(x, pre_g, post_g, wg, wu, wd):
    n = x.shape[0]
    tm, tf = min(FFN_TM, n), FFN_TF
    grid = (n // tm, FFN_DIM // tf)
    row = pl.BlockSpec((tm, D_MODEL), lambda i, j: (i, 0))
    vec = pl.BlockSpec((1, D_MODEL), lambda i, j: (0, 0))
    return pl.pallas_call(
        functools.partial(_ffn_kernel, n_f=grid[1]),
        grid=grid,
        in_specs=[row, vec, vec,
                  pl.BlockSpec((D_MODEL, tf), lambda i, j: (0, j)),
                  pl.BlockSpec((D_MODEL, tf), lambda i, j: (0, j)),
                  pl.BlockSpec((tf, D_MODEL), lambda i, j: (j, 0))],
        out_specs=row,
        out_shape=jax.ShapeDtypeStruct((n, D_MODEL), F32),
        scratch_shapes=[pltpu.VMEM((tm, D_MODEL), BF16)],
        compiler_params=_params(("parallel", "arbitrary")),
        name="ffn",
    )(x, pre_g, post_g, wg, wu, wd)


def _mixin_kernel(xc_ref, xp_ref, xn_ref, win_ref, poolw_ref, wup_f_ref, aup_f_ref, wup_b_ref, aup_b_ref, vec_ref,
                  pool_o, v_o, bonus_o, sg_o,
                  at_f, rt_f, bt_f, kt_f, bp_f, kp_f, pe_f,
                  at_b, rt_b, bt_b, kt_b, bp_b, kp_b, pe_b,
                  u_ref, ext_ref, pw_ref, *, seq_len, n_tiles):
    t_tile = xc_ref.shape[0]
    n_chunks = t_tile // CHUNK
    i = pl.program_id(1)
    views, off = {}, 0
    for name, width in MIXIN_VECS:
        views[name] = vec_ref.at[:, off:off + width]
        off += width
    preg_ref, mu_ref, pscale_ref = views["pre_g"], views["mu"], views["pool_scale"]
    w0f_ref, a0f_ref, w0b_ref, a0b_ref = views["w0_f"], views["a0_f"], views["w0_b"], views["a0_b"]
    kk_ref, ka_ref, rk_ref = views["k_k"], views["k_a"], views["r_k"]

    x_ext = jnp.concatenate([jnp.where(i > 0, xp_ref[...], 0.0), xc_ref[...],
                             jnp.where(i < n_tiles - 1, xn_ref[...], 0.0)], axis=0)
    u_ref[...] = _rms(x_ext, preg_ref[...]).astype(BF16)

    units = [(gi * POOL_GROUP, POOL_GROUP) for gi in range(len(POOL_WINDOWS))] + [(LORA_COL, LORA_PAD)]
    n_head = len(units)
    for c in range(0, RWKV_WIDTH, MXU_TILE):
        units += [(POOL_WIDTH + s * RWKV_WIDTH + c, MXU_TILE) for s in range(3)]
    issued = [0]

    def ensure(n):
        while issued[0] < min(n, len(units)):
            col, width = units[issued[0]]
            ext_ref[0:t_tile + 2 * HALO, col:col + width] = _dot(u_ref[...], win_ref[:, col:col + width])
            issued[0] += 1

    def pair_units(p):
        return n_head + 3 * (p * LANES // MXU_TILE + 1)

    def rows(d, col, width):
        return ext_ref[HALO + d:HALO + d + t_tile, col:col + width]

    ensure(2)

    pos = i * t_tile + lax.broadcasted_iota(jnp.int32, (t_tile, 1), 0)
    n_ext = t_tile + 2 * HALO
    zero_rows = jnp.zeros((HALO, POOL_GROUP), F32)
    for buf in (0, 1):
        pw_ref[buf, 0:HALO, :] = zero_rows
        pw_ref[buf, HALO + n_ext:, :] = zero_rows

    def window_sum(col, win):
        half = win // 2
        if win <= 4:
            tot = rows(-half, col, POOL_GROUP)
            for d in range(-half + 1, win - half):
                tot = tot + rows(d, col, POOL_GROUP)
            return tot
        cs = slice(col, col + POOL_GROUP)
        ext_ref[n_ext:n_ext + HALO, cs] = zero_rows
        ext_rows = pl.ds(HALO, n_ext)
        pw_ref[0, ext_rows, :] = ext_ref[0:n_ext, cs] + ext_ref[1:n_ext + 1, cs]
        pw_ref[1, ext_rows, :] = pw_ref[0, HALO - 2:HALO - 2 + n_ext, :] + pw_ref[0, ext_rows, :]
        if win == 8:
            return rows_pw(1, -2) + rows_pw(1, 2)
        pw_ref[0, ext_rows, :] = (pw_ref[1, HALO - 2:HALO - 2 + n_ext, :]
                                  + pw_ref[1, HALO + 2:HALO + 2 + n_ext, :])
        return rows_pw(0, -4) + rows_pw(0, 4)

    def rows_pw(buf, d):
        return pw_ref[buf, 2 * HALO + d:2 * HALO + d + t_tile, :]

    for gi, win in enumerate(POOL_WINDOWS):
        half = win // 2
        col = gi * POOL_GROUP
        tot = window_sum(col, win)
        cnt = (jnp.minimum(pos + (win - half), seq_len) - jnp.maximum(pos - half, 0)).astype(F32)
        pooled = tot / cnt - rows(0, col, POOL_GROUP)
        ensure(gi + 3)
        mixed = _dot(pooled.astype(BF16), poolw_ref[gi])
        pool_o[:, col:col + POOL_GROUP] = (mixed * pscale_ref[:, col:col + POOL_GROUP]).astype(BF16)

    def shifted(col, width):
        mu = mu_ref[:, col - POOL_WIDTH:col - POOL_WIDTH + width]
        return (1.0 - mu) * rows(0, col, width) + (0.5 * mu) * (rows(-1, col, width) + rows(1, col, width))

    ensure(pair_units(0))
    lora = shifted(LORA_COL, LORA_PAD)
    dwa = lora[:, 0:LANES]
    tw = jnp.tanh(dwa).astype(BF16)
    da = dwa.astype(BF16)
    sg_o[...] = _sigmoid(lora[:, LANES:]).astype(BF16)

    rr = lax.broadcasted_iota(jnp.int32, (t_tile, t_tile), 0)
    cc = lax.broadcasted_iota(jnp.int32, (t_tile, t_tile), 1)
    same = (rr // CHUNK) == (cc // CHUNK)
    m_fwd = jnp.where(same & (cc <= rr), 1.0, 0.0).astype(BF16)
    m_bwd = jnp.where(same & (cc >= rr), 1.0, 0.0).astype(BF16)

    dirs = ((w0f_ref, a0f_ref, m_fwd, CHUNK - 1, (at_f, rt_f, bt_f, kt_f, bp_f, kp_f, pe_f)),
            (w0b_ref, a0b_ref, m_bwd, 0, (at_b, rt_b, bt_b, kt_b, bp_b, kp_b, pe_b)))

    for p in range(N_PAIRS):
        ls = slice(p * LANES, (p + 1) * LANES)
        ensure(pair_units(p))

        def trickle(p=p):
            ensure(min(issued[0] + 1, pair_units(p + 1)))

        col = POOL_WIDTH + p * LANES
        r = shifted(col, LANES)
        k = shifted(col + RWKV_WIDTH, LANES)
        v = shifted(col + 2 * RWKV_WIDTH, LANES)
        v_o[:, ls] = v.astype(BF16)

        kkr = k * kk_ref[:, ls]
        trickle()
        nrm = jnp.sqrt(_head_sum(kkr * kkr))
        kk = kkr / jnp.maximum(nrm, 1e-12)
        nkk = -kk

        xw_fb = _dot(tw, jnp.concatenate([wup_f_ref[:, ls], wup_b_ref[:, ls]], axis=1))
        xa_fb = _dot(da, jnp.concatenate([aup_f_ref[:, ls], aup_b_ref[:, ls]], axis=1))

        ksum = None
        for di, (w0_ref, a0_ref, m_dir, end_row, outs) in enumerate(dirs):
            o_at, o_rt, o_bt, o_kt, o_bp, o_kp, o_pe = outs
            dl = slice(di * LANES, (di + 1) * LANES)
            xw = w0_ref[:, ls] + xw_fb[:, dl]
            lw2 = -DECAY_SCALE2 * _sigmoid(xw)
            a = _sigmoid(a0_ref[:, ls] + xa_fb[:, dl])
            kd = k * (1.0 + (a - 1.0) * ka_ref[:, ls])
            ksum = kd if ksum is None else ksum + kd
            beta = kk * a
            trickle()
            lp2 = _dot(m_dir, jnp.concatenate(_split2(lw2), axis=1))
            lp = lp2[:, 0:LANES] + lp2[:, LANES:]
            ends = [lp[c * CHUNK + end_row:c * CHUNK + end_row + 1, :] for c in range(n_chunks)]
            lpe = jnp.concatenate([jnp.broadcast_to(e, (CHUNK, LANES)) for e in ends], axis=0)
            e_inv = jnp.exp2(-lp)
            e_end = jnp.exp2(lpe - lp)
            o_at[:, ls] = (jnp.exp2(lp - lw2) * nkk).astype(BF16)
            o_rt[:, ls] = (jnp.exp2(lp) * r).astype(BF16)
            o_bt[:, ls] = (e_inv * beta).astype(BF16)
            o_kt[:, ls] = (e_inv * kd).astype(BF16)
            o_bp[:, ls] = (e_end * beta).astype(BF16)
            o_kp[:, ls] = (e_end * kd).astype(BF16)
            for c in range(n_chunks):
                o_pe[c, :, ls] = jnp.exp2(ends[c])

        bonus_o[:, ls] = _head_sum(r * ksum * rk_ref[:, ls]) * v


def _mixin(x, wts, seq_len):
    b = x.shape[0]
    t = min(MIXIN_T, seq_len)
    n_tiles = seq_len // t
    n_halo = t // HALO
    last_halo = seq_len // HALO - 1
    cur = pl.BlockSpec((None, t, D_MODEL), lambda bi, i: (bi, i, 0))
    prev = pl.BlockSpec((None, HALO, D_MODEL), lambda bi, i: (bi, jnp.maximum(i * n_halo - 1, 0), 0))
    nxt = pl.BlockSpec((None, HALO, D_MODEL), lambda bi, i: (bi, jnp.minimum((i + 1) * n_halo, last_halo), 0))

    def const(shape):
        nd = len(shape)
        return pl.BlockSpec(shape, lambda bi, i: (0,) * nd, pipeline_mode=pl.Buffered(1))

    tok = pl.BlockSpec((None, t, RWKV_WIDTH), lambda bi, i: (bi, i, 0))
    pe = pl.BlockSpec((None, t // CHUNK, 1, RWKV_WIDTH), lambda bi, i: (bi, i, 0, 0))
    tok_bf = jax.ShapeDtypeStruct((b, seq_len, RWKV_WIDTH), BF16)
    pe_shape = jax.ShapeDtypeStruct((b, seq_len // CHUNK, 1, RWKV_WIDTH), F32)
    lora_w = const((LANES, RWKV_WIDTH))
    in_specs = [cur, prev, nxt, const((D_MODEL, Z_COLS)), const((4, POOL_GROUP, POOL_GROUP)),
                lora_w, lora_w, lora_w, lora_w, const((1, MIXIN_VEC_PAD))]
    out_specs = [pl.BlockSpec((None, t, POOL_WIDTH), lambda bi, i: (bi, i, 0)), tok, tok,
                 pl.BlockSpec((None, t, GATE_PAD), lambda bi, i: (bi, i, 0))] + 2 * ([tok] * 6 + [pe])
    out_shape = [jax.ShapeDtypeStruct((b, seq_len, POOL_WIDTH), BF16), tok_bf,
                 jax.ShapeDtypeStruct((b, seq_len, RWKV_WIDTH), F32),
                 jax.ShapeDtypeStruct((b, seq_len, GATE_PAD), BF16)] + 2 * ([tok_bf] * 6 + [pe_shape])
    return pl.pallas_call(
        functools.partial(_mixin_kernel, seq_len=seq_len, n_tiles=n_tiles),
        grid=(b, n_tiles),
        in_specs=in_specs,
        out_specs=out_specs,
        out_shape=out_shape,
        scratch_shapes=[pltpu.VMEM((t + 2 * HALO, D_MODEL), BF16),
                        pltpu.VMEM((-(-(t + 2 * HALO) // EXT_ROW_MULT) * EXT_ROW_MULT, Z_COLS), F32),
                        pltpu.VMEM((2, t + 4 * HALO, POOL_GROUP), F32)],
        compiler_params=_params(("parallel", "arbitrary")),
        name="mixin",
    )(x, x, x, *wts)


def _scan_kernel(v_f, at_f, rt_f, bt_f, kt_f, bp_f, kp_f, pe_f,
                 v_b, at_b, rt_b, bt_b, kt_b, bp_b, kp_b, pe_b,
                 y_f, y_b, s_ref):
    n_chunks = v_f.shape[0] // CHUNK

    @pl.when(pl.program_id(1) == 0)
    def _():
        s_ref[...] = jnp.zeros_like(s_ref)

    lane = lax.broadcasted_iota(jnp.int32, (CHUNK, LANES), 1)
    head0 = lane < HEAD_SIZE
    tt = lax.broadcasted_iota(jnp.int32, (CHUNK, LANES), 0)
    ss = lane % HEAD_SIZE
    r2 = lax.broadcasted_iota(jnp.int32, (LANES, LANES), 0)
    c2 = lax.broadcasted_iota(jnp.int32, (LANES, LANES), 1)
    bd = (r2 // HEAD_SIZE) == (c2 // HEAD_SIZE)
    eye_h = jnp.where(tt == ss, 1.0, 0.0).astype(F32)
    n_double = int(math.log2(CHUNK)) - 1

    def stack2(x):
        zero = jnp.zeros_like(x)
        return jnp.concatenate([jnp.where(head0, x, zero), jnp.where(head0, zero, x)], axis=0)

    fwd_refs = (v_f, at_f, rt_f, bt_f, kt_f, bp_f, kp_f, pe_f)
    bwd_refs = (v_b, at_b, rt_b, bt_b, kt_b, bp_b, kp_b, pe_b)

    def blk(m_h):
        return jnp.where(bd, jnp.concatenate([m_h, m_h], axis=0), jnp.zeros((), m_h.dtype))

    steps = []
    for k in range(n_chunks):
        chains = []
        for d, refs, y_ref, c, strict, incl in ((0, fwd_refs, y_f, k, ss < tt, ss <= tt),
                                                (1, bwd_refs, y_b, n_chunks - 1 - k, ss > tt, ss >= tt)):
            rs = slice(c * CHUNK, (c + 1) * CHUNK)
            for p in range(N_PAIRS):
                ls = slice(p * LANES, (p + 1) * LANES)
                chains.append(dict(d=d, p=p, rs=rs, ls=ls, c=c, refs=refs, y_ref=y_ref, strict=strict, incl=incl))
        steps.append(chains)
    every = [ch for chains in steps for ch in chains]

    for ch in every:
        v_r, at_r, rt_r, bt_r, kt_r, _, _, _ = ch["refs"]
        rs, ls = ch["rs"], ch["ls"]
        hh = _dot_nt(jnp.concatenate([at_r[rs, ls], rt_r[rs, ls]], axis=0),
                     jnp.concatenate([stack2(bt_r[rs, ls]), stack2(kt_r[rs, ls])], axis=0))
        h_ab = jnp.where(ch["strict"], hh[0:CHUNK, 0:LANES], 0.0)
        ch["h_kv"] = jnp.concatenate([jnp.where(ch["strict"], hh[0:CHUNK, LANES:], 0.0),
                                      jnp.where(ch["incl"], hh[CHUNK:, LANES:], 0.0)], axis=0).astype(BF16)
        ch["h_rb"] = jnp.where(ch["incl"], hh[CHUNK:, 0:LANES], 0.0).astype(BF16)
        ch["x"] = h_ab.astype(BF16)
        ch["t"] = eye_h + h_ab

    for ch in every:
        ch["m2"] = _dot(ch["h_kv"], stack2(ch["refs"][0][ch["rs"], ch["ls"]]))
        ch["x"] = _dot(ch["x"], blk(ch["x"])).astype(BF16)

    for k in range(n_double):
        for ch in every:
            x_blk = blk(ch["x"])
            if k + 1 < n_double:
                tx = _dot(jnp.concatenate([ch["t"].astype(BF16), ch["x"]], axis=0), x_blk)
                ch["t"] = ch["t"] + tx[0:CHUNK]
                ch["x"] = tx[CHUNK:].astype(BF16)
            else:
                ch["t"] = (ch["t"] + _dot(ch["t"].astype(BF16), x_blk)).astype(BF16)

    state = {(ch["d"], ch["p"]): s_ref[ch["d"], ch["p"]] for ch in steps[0]}
    for chains in steps:
        for ch in chains:
            _, at_r, rt_r = ch["refs"][:3]
            rs, ls = ch["rs"], ch["ls"]
            st_b = state[ch["d"], ch["p"]].astype(BF16)
            m1 = _dot(jnp.concatenate([at_r[rs, ls], rt_r[rs, ls]], axis=0), st_b)
            ch["w"] = (m1[0:CHUNK] + ch["m2"][0:CHUNK]).astype(BF16)
            ch["y0"] = m1[CHUNK:] + ch["m2"][CHUNK:]
        for ch in chains:
            ch["ub"] = _dot(ch["t"], stack2(ch["w"])).astype(BF16)
        for ch in chains:
            v_r, _, _, _, _, bp_r, kp_r, pe_r = ch["refs"]
            rs, ls = ch["rs"], ch["ls"]
            ch["y_ref"][rs, ls] = ch["y0"] + _dot(ch["h_rb"], stack2(ch["ub"]))
            upd = _dot_tn(jnp.concatenate([bp_r[rs, ls], kp_r[rs, ls]], axis=0),
                          jnp.concatenate([ch["ub"], v_r[rs, ls]], axis=0))
            pend_col = jnp.broadcast_to(pe_r[ch["c"]][:, ls], (LANES, LANES)).T
            state[ch["d"], ch["p"]] = state[ch["d"], ch["p"]] * pend_col + jnp.where(bd, upd, 0.0)

    for (d, p), st in state.items():
        s_ref[d, p] = st


def _scan(v, fwd, bwd, seq_len):
    b = v.shape[0]
    t = min(SCAN_T, seq_len)
    n_tiles = seq_len // t
    nc = t // CHUNK

    def tok(rev):
        if rev:
            return pl.BlockSpec((None, t, RWKV_WIDTH), lambda bi, i: (bi, n_tiles - 1 - i, 0))
        return pl.BlockSpec((None, t, RWKV_WIDTH), lambda bi, i: (bi, i, 0))

    def pe(rev):
        if rev:
            return pl.BlockSpec((None, nc, 1, RWKV_WIDTH), lambda bi, i: (bi, n_tiles - 1 - i, 0, 0))
        return pl.BlockSpec((None, nc, 1, RWKV_WIDTH), lambda bi, i: (bi, i, 0, 0))

    in_specs = [tok(False)] * 7 + [pe(False)] + [tok(True)] * 7 + [pe(True)]
    y_shape = jax.ShapeDtypeStruct((b, seq_len, RWKV_WIDTH), F32)
    return pl.pallas_call(
        _scan_kernel,
        grid=(b, n_tiles),
        in_specs=in_specs,
        out_specs=[tok(False), tok(True)],
        out_shape=[y_shape, y_shape],
        scratch_shapes=[pltpu.VMEM((2, N_PAIRS, LANES, LANES), F32)],
        compiler_params=_params(("parallel", "arbitrary")),
        name="scan",
    )(v, *fwd, v, *bwd)


def _post_kernel(x_ref, yf_ref, yb_ref, bonus_ref, sg_ref, pool_ref, lnw_ref, lnb_ref, gup_ref,
                 wout_ref, postg_ref, o_ref, mix_ref):
    inv_n = 1.0 / HEAD_SIZE
    gate = _dot(sg_ref[...], gup_ref[...])
    acc = _dot(pool_ref[...], wout_ref[0:POOL_WIDTH, :])
    for p in range(N_PAIRS):
        ls = slice(p * LANES, (p + 1) * LANES)
        y = yf_ref[:, ls] + yb_ref[:, ls]
        mean = _head_sum(y) * inv_n
        yc = y - mean
        var = _head_sum(yc * yc) * inv_n
        yn = yc * lax.rsqrt(var + LNX_EPS) * lnw_ref[:, ls] + lnb_ref[:, ls]
        out = (yn + bonus_ref[:, ls]) * gate[:, ls]
        mix_ref[:, ls] = out.astype(BF16)
        if (p + 1) * LANES % MXU_TILE == 0:
            ks = slice((p + 1) * LANES - MXU_TILE, (p + 1) * LANES)
            acc = acc + _dot(mix_ref[:, ks], wout_ref[POOL_WIDTH + ks.start:POOL_WIDTH + ks.stop, :])
    o_ref[...] = acc
    _residual_norm_inplace(o_ref, x_ref, postg_ref[...])


def _post(x, yf, yb, bonus, sg, pool, lnw, lnb, gup, wout, postg):
    n = x.shape[0]
    tm = min(POST_TM, n)
    row = pl.BlockSpec((tm, D_MODEL), lambda i: (i, 0))
    half = pl.BlockSpec((tm, RWKV_WIDTH), lambda i: (i, 0))
    vec_h = pl.BlockSpec((1, RWKV_WIDTH), lambda i: (0, 0))
    return pl.pallas_call(
        _post_kernel,
        grid=(n // tm,),
        in_specs=[row, half, half, half, pl.BlockSpec((tm, GATE_PAD), lambda i: (i, 0)), half, vec_h, vec_h,
                  pl.BlockSpec((GATE_PAD, RWKV_WIDTH), lambda i: (0, 0), pipeline_mode=pl.Buffered(1)),
                  pl.BlockSpec((D_MODEL, D_MODEL), lambda i: (0, 0), pipeline_mode=pl.Buffered(1)),
                  pl.BlockSpec((1, D_MODEL), lambda i: (0, 0))],
        out_specs=row,
        out_shape=jax.ShapeDtypeStruct((n, D_MODEL), F32),
        scratch_shapes=[pltpu.VMEM((tm, RWKV_WIDTH), BF16)],
        compiler_params=_params(("parallel",)),
        name="post",
    )(x, yf, yb, bonus, sg, pool, lnw, lnb, gup, wout, postg)


def _ple_kernel(x_ref, p_ref, preg_ref, postg_ref, wgate_ref, wproj_ref, o_ref):
    u = _rms(x_ref[...], preg_ref[...]).astype(BF16)
    p = p_ref[...].astype(BF16)
    for c in range(0, D_MODEL, PLE_TN):
        cs = slice(c, c + PLE_TN)
        o_ref[:, cs] = _sigmoid(_dot(u, wgate_ref[:, cs])) * _dot(p, wproj_ref[:, cs])
    _residual_norm_inplace(o_ref, x_ref, postg_ref[...])


def _ple(x, p, preg, postg, wgate, wproj):
    n = x.shape[0]
    tm = min(PLE_TM, n)
    row = pl.BlockSpec((tm, D_MODEL), lambda i: (i, 0))
    vec = pl.BlockSpec((1, D_MODEL), lambda i: (0, 0))
    return pl.pallas_call(
        _ple_kernel,
        grid=(n // tm,),
        in_specs=[row, pl.BlockSpec((tm, PLE_DIM), lambda i: (i, 0)), vec, vec,
                  pl.BlockSpec((D_MODEL, D_MODEL), lambda i: (0, 0)),
                  pl.BlockSpec((PLE_DIM, D_MODEL), lambda i: (0, 0))],
        out_specs=row,
        out_shape=jax.ShapeDtypeStruct((n, D_MODEL), F32),
        compiler_params=_params(("parallel",)),
        name="ple",
    )(x, p, preg, postg, wgate, wproj)


def _pad_rows(w, before, total):
    return jnp.pad(w, ((before, total - before - w.shape[0]), (0, 0)))


def _prepare_weights(ffn1_pre_g, ffn1_post_g, ffn1_w_gate, ffn1_w_up, ffn1_w_down,
                     mix_pre_g, mix_post_g, w_in, mu_shift, pool_w, pool_scale,
                     w0_f, w_up_f, a0_f, a_up_f, w0_b, w_up_b, a0_b, a_up_b,
                     g_up, k_k, k_a, r_k, lnx_w, lnx_b, w_out,
                     ffn2_pre_g, ffn2_post_g, ffn2_w_gate, ffn2_w_up, ffn2_w_down,
                     ple_pre_g, ple_post_g, ple_gate_w, ple_proj_w):
    row = lambda a: a.reshape(1, -1)
    bf = lambda a: a.astype(BF16)
    pad_c = Z_COLS - w_in.shape[1]

    vecs = dict(pre_g=mix_pre_g, mu=jnp.pad(mu_shift, (0, pad_c)), pool_scale=pool_scale,
                w0_f=w0_f, a0_f=a0_f, w0_b=w0_b, a0_b=a0_b, k_k=k_k, k_a=k_a, r_k=r_k.reshape(-1))
    packed = jnp.concatenate([vecs[name] for name, _ in MIXIN_VECS])
    mixin = (bf(jnp.pad(w_in, ((0, 0), (0, pad_c)))), bf(pool_w),
             bf(_pad_rows(w_up_f, 0, LANES)), bf(_pad_rows(a_up_f, DECAY_LORA, LANES)),
             bf(_pad_rows(w_up_b, 0, LANES)), bf(_pad_rows(a_up_b, DECAY_LORA, LANES)),
             row(jnp.pad(packed, (0, MIXIN_VEC_PAD - packed.shape[0]))))
    return dict(
        ffn1=(row(ffn1_pre_g), row(ffn1_post_g), bf(ffn1_w_gate), bf(ffn1_w_up), bf(ffn1_w_down)),
        mixin=mixin,
        post=(row(lnx_w), row(lnx_b), bf(_pad_rows(g_up, 0, GATE_PAD)), bf(w_out), row(mix_post_g)),
        ffn2=(row(ffn2_pre_g), row(ffn2_post_g), bf(ffn2_w_gate), bf(ffn2_w_up), bf(ffn2_w_down)),
        ple=(row(ple_pre_g), row(ple_post_g), bf(ple_gate_w), bf(ple_proj_w)),
    )


def _layer(x, p, wts):
    b, seq_len, _ = x.shape
    n = b * seq_len
    x = _ffn(x.reshape(n, D_MODEL), *wts["ffn1"])
    outs = _mixin(x.reshape(b, seq_len, D_MODEL), wts["mixin"], seq_len)
    pool, v, bonus, sg = outs[:4]
    fwd, bwd = outs[4:11], outs[11:18]
    yf, yb = _scan(v, fwd, bwd, seq_len)
    flat = lambda a: a.reshape(n, a.shape[-1])
    x = _post(x, flat(yf), flat(yb), flat(bonus), flat(sg), flat(pool), *wts["post"])
    x = _ffn(x, *wts["ffn2"])
    x = _ple(x, p.reshape(n, PLE_DIM), *wts["ple"])
    return x.reshape(b, seq_len, D_MODEL)


def kernel(x_prompt, x_sample, p_prompt, p_sample, ffn1_pre_g, ffn1_post_g, ffn1_w_gate, ffn1_w_up, ffn1_w_down, mix_pre_g, mix_post_g, w_in, mu_shift, pool_w, pool_scale, w0_f, w_up_f, a0_f, a_up_f, w0_b, w_up_b, a0_b, a_up_b, g_up, k_k, k_a, r_k, lnx_w, lnx_b, w_out, ffn2_pre_g, ffn2_post_g, ffn2_w_gate, ffn2_w_up, ffn2_w_down, ple_pre_g, ple_post_g, ple_gate_w, ple_proj_w):
    weights = (ffn1_pre_g, ffn1_post_g, ffn1_w_gate, ffn1_w_up, ffn1_w_down, mix_pre_g, mix_post_g, w_in,
               mu_shift, pool_w, pool_scale, w0_f, w_up_f, a0_f, a_up_f, w0_b, w_up_b, a0_b, a_up_b,
               g_up, k_k, k_a, r_k, lnx_w, lnx_b, w_out, ffn2_pre_g, ffn2_post_g, ffn2_w_gate, ffn2_w_up,
               ffn2_w_down, ple_pre_g, ple_post_g, ple_gate_w, ple_proj_w)
    y_prompt, y_sample = x_prompt, x_sample
    for layer in range(ffn1_pre_g.shape[0]):
        wts = _prepare_weights(*(w[layer] for w in weights))
        y_prompt = _layer(y_prompt, p_prompt[layer], wts)
        y_sample = _layer(y_sample, p_sample[layer], wts)
    return (y_prompt, y_sample)
```
